```python
import math
import jax
import jax.numpy as jnp
from jax import lax
import numpy as np

D_MODEL = 4096
BATCH = 2
SEQ = 8192
DEPTH = 2

GRID_W = 64
CTX_LEN = 256
HEAD_DIM = 128
N_DIR = 2
GDN_HEADS = 12
GDN_WIDTH = GDN_HEADS * HEAD_DIM
GDN_CONV_W = 5
GDN_CHUNK = 64
FN_GROUPS = 8
FN_GROUP_DIM = 128
FN_WIDTH = FN_GROUPS * FN_GROUP_DIM
GQA_HEADS = 12
GQA_KV_HEADS = 4
GQA_GROUP = GQA_HEADS // GQA_KV_HEADS
GQA_WIDTH = GQA_HEADS * HEAD_DIM
GQA_KV_WIDTH = GQA_KV_HEADS * HEAD_DIM
Q_BLOCK = 128
ROPE_THETA = 10000.0
MIX_WIDTH = GDN_WIDTH + FN_WIDTH + GQA_WIDTH
IN_SPLITS = (3 * GDN_WIDTH, GDN_WIDTH, N_DIR * GDN_HEADS, N_DIR * GDN_HEADS,
             FN_WIDTH, GQA_WIDTH, GQA_KV_WIDTH, GQA_KV_WIDTH)
N_IN = sum(IN_SPLITS)
MOE_GROUPS = 4
MOE_EXPERTS_PER_GROUP = 8
MOE_EXPERTS = MOE_GROUPS * MOE_EXPERTS_PER_GROUP
MOE_TOP_K = 2
MOE_D_FF = 512
DEEPNORM_ALPHA = (2 * DEPTH) ** 0.25
DEEPNORM_BETA = (8 * DEPTH) ** -0.25
LN_EPS = 1e-5
RMS_EPS = 1e-6
F32 = jnp.float32

kernel_name = 'hybrid_gdn_fnet_gqa_hmoe_dit'


def _layernorm(x, g, b):
    xf = x.astype(F32)
    mu = jnp.mean(xf, -1, keepdims=True)
    var = jnp.mean(jnp.square(xf - mu), -1, keepdims=True)
    return ((xf - mu) * lax.rsqrt(var + LN_EPS)).astype(x.dtype) * g + b


def _rmsnorm(x, w):
    xf = x.astype(F32)
    return (xf * lax.rsqrt(jnp.mean(xf * xf, -1, keepdims=True) + RMS_EPS)).astype(x.dtype) * w


def _l2norm(x):
    xf = x.astype(F32)
    return (xf * lax.rsqrt(jnp.sum(xf * xf, -1, keepdims=True) + RMS_EPS)).astype(x.dtype)


def _split_in(p):
    return jnp.split(p, np.cumsum(IN_SPLITS)[:-1].tolist(), axis=-1)


def _axial_rope(rows):
    row = jnp.repeat(jnp.arange(rows, dtype=F32), GRID_W)
    col = jnp.tile(jnp.arange(GRID_W, dtype=F32), rows)
    axis_dim = HEAD_DIM // 2
    inv = ROPE_THETA ** (-jnp.arange(0, axis_dim, 2, dtype=F32) / axis_dim)
    ang = jnp.concatenate([row[:, None] * inv, col[:, None] * inv], -1)
    return jnp.cos(ang), jnp.sin(ang)


def _rope(x, cos, sin):
    x1, x2 = jnp.split(x, 2, axis=-1)
    cos = cos[:, None, :].astype(x.dtype)
    sin = sin[:, None, :].astype(x.dtype)
    return jnp.concatenate([x1 * cos - x2 * sin, x2 * cos + x1 * sin], -1)


def _short_conv(u, w):
    pad = GDN_CONV_W // 2
    return lax.conv_general_dilated(u, w[:, None, :].astype(u.dtype), window_strides=(1,),
                                    padding=[(pad, pad)], dimension_numbers=('NWC', 'WIO', 'NWC'),
                                    feature_group_count=u.shape[-1])


def _gdn_inputs(qkv, a, b, conv_w, a_log, dt_bias):
    B, T, _ = qkv.shape
    r = jax.nn.silu(_short_conv(qkv, conv_w)).reshape(B, T, 3, GDN_HEADS, HEAD_DIM)
    q = _l2norm(r[:, :, 0]) * HEAD_DIM ** -0.5
    k = _l2norm(r[:, :, 1])
    v = r[:, :, 2]
    a = a.reshape(B, T, N_DIR, GDN_HEADS).astype(F32)
    g = -jnp.exp(a_log) * jax.nn.softplus(a + dt_bias)
    beta = jax.nn.sigmoid(b.reshape(B, T, N_DIR, GDN_HEADS).astype(F32))
    return q, k, v, g, beta


def _gated_delta_chunked(q, k, v, g, beta, s0):
    B, T, H, _ = q.shape
    dv = v.shape[-1]
    n = T // GDN_CHUNK

    def chunks(t):
        t = t.astype(F32).reshape(B, n, GDN_CHUNK, H, *t.shape[3:])
        return jnp.moveaxis(t, (1, 3), (0, 2))

    qf, kf, vf, bt = chunks(q), chunks(k), chunks(v), chunks(beta)
    gc = jnp.cumsum(chunks(g), axis=-1)
    idx = jnp.arange(GDN_CHUNK)
    causal = idx[:, None] >= idx[None, :]
    strict = idx[:, None] > idx[None, :]
    decay = jnp.exp(jnp.where(causal, gc[..., :, None] - gc[..., None, :], -jnp.inf))
    kb = kf * bt[..., None]
    a_mat = jnp.where(strict, jnp.einsum('nbhcd,nbhsd->nbhcs', kb, kf) * decay, 0.0)
    rhs = jnp.concatenate([vf * bt[..., None], kb * jnp.exp(gc)[..., None]], -1)
    sol = lax.linalg.triangular_solve(jnp.eye(GDN_CHUNK, dtype=F32) + a_mat, rhs,
                                      left_side=True, lower=True, unit_diagonal=True)
    u, w = sol[..., :dv], sol[..., dv:]
    qk = jnp.einsum('nbhcd,nbhsd->nbhcs', qf, kf) * decay

    def step(S, xs):
        qi, ki, ui, wi, gi, qki = xs
        v_new = ui - jnp.einsum('bhcd,bhde->bhce', wi, S)
        o = (jnp.einsum('bhcd,bhde->bhce', qi * jnp.exp(gi)[..., None], S)
             + jnp.einsum('bhcs,bhse->bhce', qki, v_new))
        g_last = gi[..., -1:]
        S = (S * jnp.exp(g_last)[..., None]
             + jnp.einsum('bhcd,bhce->bhde', ki * jnp.exp(g_last - gi)[..., None], v_new))
        return S, o

    S, o = lax.scan(step, s0.astype(F32), (qf, kf, u, w, gc, qk))
    o = jnp.moveaxis(o, (0, 2), (1, 3)).reshape(B, T, H, dv)
    return o.astype(v.dtype), S


def _flip(t, d):
    return t[:, ::-1] if d == 1 else t


def _gdn_bidirectional(lat, ctx):
    ql, kl, vl, gl, bl = lat
    qc, kc, vc, gcx, bc = ctx
    B = ql.shape[0]
    outs_l, outs_c = [], []
    for d in range(N_DIR):
        s0 = jnp.zeros((B, GDN_HEADS, HEAD_DIM, HEAD_DIM), F32)
        oc, s_ctx = _gated_delta_chunked(_flip(qc, d), _flip(kc, d), _flip(vc, d),
                                         _flip(gcx[:, :, d], d), _flip(bc[:, :, d], d), s0)
        ol, _ = _gated_delta_chunked(_flip(ql, d), _flip(kl, d), _flip(vl, d),
                                     _flip(gl[:, :, d], d), _flip(bl[:, :, d], d), s_ctx)
        outs_l.append(_flip(ol, d))
        outs_c.append(_flip(oc, d))
    return outs_l[0] + outs_l[1], outs_c[0] + outs_c[1]


def _gdn_output(o, z, norm_w):
    B, T = o.shape[:2]
    y = _rmsnorm(o, norm_w) * jax.nn.silu(z.reshape(B, T, GDN_HEADS, HEAD_DIM))
    return y.reshape(B, T, GDN_WIDTH)


def _fourier_mix(f, fn_w):
    B, T, _ = f.shape
    fr = jnp.fft.fft2(f.astype(F32).reshape(B, T, FN_GROUPS, FN_GROUP_DIM), axes=(1, 3), norm='ortho').real
    return fr.astype(f.dtype).reshape(B, T, FN_WIDTH) @ fn_w


def _gqa_heads(q, k, v, q_norm_w, k_norm_w):
    B, T, _ = q.shape
    q = _rmsnorm(q.reshape(B, T, GQA_HEADS, HEAD_DIM), q_norm_w)
    k = _rmsnorm(k.reshape(B, T, GQA_KV_HEADS, HEAD_DIM), k_norm_w)
    return q, k, v.reshape(B, T, GQA_KV_HEADS, HEAD_DIM)


def _attend(q, k, v):
    B, Tq = q.shape[:2]
    qg = q.reshape(B, Tq, GQA_KV_HEADS, GQA_GROUP, HEAD_DIM)
    s = jnp.einsum('bqkgd,bskd->bkgqs', qg, k).astype(F32) * HEAD_DIM ** -0.5
    p = jax.nn.softmax(s, axis=-1).astype(v.dtype)
    return jnp.einsum('bkgqs,bskd->bqkgd', p, v).reshape(B, Tq, GQA_WIDTH)


def _attend_blocked(q, k, v):
    B, T = q.shape[:2]
    nb = T // Q_BLOCK
    qb = jnp.moveaxis(q.reshape(B, nb, Q_BLOCK, GQA_HEADS, HEAD_DIM), 1, 0)
    o = lax.map(lambda qi: _attend(qi, k, v), qb)
    return jnp.moveaxis(o, 0, 1).reshape(B, T, GQA_WIDTH)


def _token_mixers(pl, pc, rope, conv_w, a_log, dt_bias, gdn_norm_w, fn_w, q_norm_w, k_norm_w, need_ctx):
    qkv, z, a, b, f, q, k, v = pl
    qkv_c, z_c, a_c, b_c, f_c, q_c, k_c, v_c = pc
    o_l, o_c = _gdn_bidirectional(_gdn_inputs(qkv, a, b, conv_w, a_log, dt_bias),
                                  _gdn_inputs(qkv_c, a_c, b_c, conv_w, a_log, dt_bias))
    gdn_l = _gdn_output(o_l, z, gdn_norm_w)
    fn_l = _fourier_mix(f, fn_w)
    ql, kl, vl = _gqa_heads(q, k, v, q_norm_w, k_norm_w)
    ql, kl = _rope(ql, *rope), _rope(kl, *rope)
    qc, kc, vc = _gqa_heads(q_c, k_c, v_c, q_norm_w, k_norm_w)
    at_l = _attend_blocked(ql, jnp.concatenate([kc, kl], 1), jnp.concatenate([vc, vl], 1))
    mix_l = jnp.concatenate([gdn_l, fn_l, at_l], -1)
    if not need_ctx:
        return mix_l, None
    mix_c = jnp.concatenate([_gdn_output(o_c, z_c, gdn_norm_w), _fourier_mix(f_c, fn_w),
                             _attend(qc, kc, vc)], -1)
    return mix_l, mix_c


def _hier_moe(h, wg, bg, we, be, w1, w3, w2):
    shp = h.shape
    h = h.reshape(-1, shp[-1])
    n = h.shape[0]
    rows = jnp.arange(n)
    lg = (h @ wg + bg).astype(F32)
    grp = jnp.argmax(lg, -1)
    p_grp = jax.nn.softmax(lg, -1)[rows, grp][:, None]
    le = (h @ we + be).astype(F32).reshape(n, MOE_GROUPS, MOE_EXPERTS_PER_GROUP)[rows, grp]
    top_v, top_i = lax.top_k(le, MOE_TOP_K)
    w_sel = p_grp * jax.nn.softmax(top_v, -1)
    eid = grp[:, None] * MOE_EXPERTS_PER_GROUP + top_i
    gates = jnp.einsum('nk,nke->ne', w_sel, jax.nn.one_hot(eid, MOE_EXPERTS, dtype=F32)).astype(h.dtype)
    out = jnp.zeros_like(h)
    for e in range(MOE_EXPERTS):
        y = (jax.nn.silu(h @ w1[e]) * (h @ w3[e])) @ w2[e]
        out = out + gates[:, e:e + 1] * y
    return out.reshape(shp)


def setup_inputs(seed: int = 0) -> dict:
    key = jax.random.key(seed)
    ks = jax.random.split(key, 32)
    L, D = DEPTH, D_MODEL

    def nrm(k, shape, scale):
        return jax.random.normal(k, shape, F32) * scale

    dt = jnp.exp(jax.random.uniform(ks[9], (L, N_DIR, GDN_HEADS), F32, math.log(1e-3), math.log(1e-1)))
    return {
        'x': nrm(ks[0], (BATCH, SEQ, D), 1.0),
        'c': nrm(ks[1], (BATCH, D), 1.0),
        'ctx': nrm(ks[2], (BATCH, CTX_LEN, D), 1.0),
        'c_ctx': nrm(ks[3], (D,), 1.0),
        'w_mod': nrm(ks[4], (L, D, 6 * D), D ** -0.5),
        'b_mod': nrm(ks[5], (L, 6 * D), 0.02),
        'w_in': nrm(ks[6], (L, D, N_IN), D ** -0.5),
        'gdn_conv': nrm(ks[7], (L, GDN_CONV_W, 3 * GDN_WIDTH), GDN_CONV_W ** -0.5),
        'gdn_a_log': jnp.log(jax.random.uniform(ks[8], (L, N_DIR, GDN_HEADS), F32, 1.0, 16.0)),
        'gdn_dt_bias': dt + jnp.log(-jnp.expm1(-dt)),
        'gdn_norm': 1.0 + nrm(ks[10], (L, HEAD_DIM), 0.02),
        'fn_w': nrm(ks[11], (L, FN_WIDTH, FN_WIDTH), FN_WIDTH ** -0.5),
        'q_norm': 1.0 + nrm(ks[12], (L, HEAD_DIM), 0.02),
        'k_norm': 1.0 + nrm(ks[13], (L, HEAD_DIM), 0.02),
        'w_out': nrm(ks[14], (L, MIX_WIDTH, D), MIX_WIDTH ** -0.5 * DEEPNORM_BETA),
        'ln1_g': 1.0 + nrm(ks[15], (L, D), 0.02),
        'ln1_b': nrm(ks[16], (L, D), 0.02),
        'ln2_g': 1.0 + nrm(ks[17], (L, D), 0.02),
        'ln2_b': nrm(ks[18], (L, D), 0.02),
        'router_g': nrm(ks[19], (L, D, MOE_GROUPS), D ** -0.5),
        'router_g_b': nrm(ks[20], (L, MOE_GROUPS), 0.01),
        'router_e': nrm(ks[21], (L, D, MOE_EXPERTS), D ** -0.5),
        'router_e_b': nrm(ks[22], (L, MOE_EXPERTS), 0.01),
        'w1': nrm(ks[23], (L, MOE_EXPERTS, D, MOE_D_FF), D ** -0.5),
        'w3': nrm(ks[24], (L, MOE_EXPERTS, D, MOE_D_FF), D ** -0.5),
        'w2': nrm(ks[25], (L, MOE_EXPERTS, MOE_D_FF, D), MOE_D_FF ** -0.5 * DEEPNORM_BETA),
    }


def reference(x, c, ctx, c_ctx, w_mod, b_mod, w_in, gdn_conv, gdn_a_log, gdn_dt_bias, gdn_norm, fn_w,
              q_norm, k_norm, w_out, ln1_g, ln1_b, ln2_g, ln2_b, router_g, router_g_b, router_e,
              router_e_b, w1, w3, w2):
    T = x.shape[1]
    n_ctx = ctx.shape[1]
    rows = T // GRID_W
    rope = _axial_rope(rows)
    sc = jax.nn.silu(c)
    scc = jax.nn.silu(c_ctx)
    xc = ctx
    for l in range(DEPTH):
        last = l == DEPTH - 1
        sh1, s1, g1, sh2, s2, g2 = (m[:, None] for m in jnp.split(sc @ w_mod[l] + b_mod[l], 6, -1))
        csh1, cs1, cg1, csh2, cs2, cg2 = jnp.split(scc @ w_mod[l] + b_mod[l], 6, -1)
        pl = _split_in((x * (1 + s1) + sh1) @ w_in[l])
        pc = _split_in((xc * (1 + cs1) + csh1) @ w_in[l])
        mix_l, mix_c = _token_mixers(pl, pc, rope, gdn_conv[l], gdn_a_log[l], gdn_dt_bias[l], gdn_norm[l],
                                     fn_w[l], q_norm[l], k_norm[l], need_ctx=not last)
        x = _layernorm(DEEPNORM_ALPHA * x + g1 * (mix_l @ w_out[l]), ln1_g[l], ln1_b[l])
        moe_w = (router_g[l], router_g_b[l], router_e[l], router_e_b[l], w1[l], w3[l], w2[l])
        if last:
            y = _hier_moe(x * (1 + s2) + sh2, *moe_w)
            x = _layernorm(DEEPNORM_ALPHA * x + g2 * y, ln2_g[l], ln2_b[l])
        else:
            xc = _layernorm(DEEPNORM_ALPHA * xc + cg1 * (mix_c @ w_out[l]), ln1_g[l], ln1_b[l])
            h_all = jnp.concatenate([xc * (1 + cs2) + csh2, x * (1 + s2) + sh2], axis=1)
            y = _hier_moe(h_all, *moe_w)
            x = _layernorm(DEEPNORM_ALPHA * x + g2 * y[:, n_ctx:], ln2_g[l], ln2_b[l])
            xc = _layernorm(DEEPNORM_ALPHA * xc + cg2 * y[:, :n_ctx], ln2_g[l], ln2_b[l])
    return x
```

```python
import functools
import math

import numpy as np
import jax
import jax.numpy as jnp
from jax import lax
from jax.experimental import pallas as pl
from jax.experimental.pallas import tpu as pltpu

F32 = jnp.float32
BF16 = jnp.bfloat16

HEAD_DIM = 128
GDN_HEADS = 12
GDN_WIDTH = GDN_HEADS * HEAD_DIM
GDN_CONV_W = 5
GDN_CHUNK = 64
GDN_INV_BLOCK = 16
FN_GROUPS = 8
FN_WIDTH = FN_GROUPS * HEAD_DIM
GQA_HEADS = 12
GQA_KV_HEADS = 4
GQA_GROUP = GQA_HEADS // GQA_KV_HEADS
GQA_WIDTH = GQA_HEADS * HEAD_DIM
GQA_KV_WIDTH = GQA_KV_HEADS * HEAD_DIM
GRID_W = 64
ROPE_THETA = 10000.0
MOE_GROUPS = 4
MOE_EPG = 8
MOE_EXPERTS = MOE_GROUPS * MOE_EPG
LN_EPS = 1e-5
RMS_EPS = 1e-6
LANES = 128
SUBLANES = 8
VMEM_LIMIT = 56 * 1024 * 1024

OFF_QKV = 0
OFF_Z = 3 * GDN_WIDTH
OFF_F = OFF_Z + GDN_WIDTH
OFF_Q = OFF_F + FN_WIDTH
OFF_K = OFF_Q + GQA_WIDTH
OFF_V = OFF_K + GQA_KV_WIDTH
N_MAIN = OFF_V + GQA_KV_WIDTH


def _cparams(sem, **kw):
    return pltpu.CompilerParams(dimension_semantics=sem, vmem_limit_bytes=VMEM_LIMIT, **kw)


def _pick(n, cands):
    for c in cands:
        if n % c == 0:
            return c
    raise ValueError(f"no tile for {n} in {cands}")


def _dot(a, b):
    return jnp.dot(a, b, preferred_element_type=F32)


def _dot_nt(a, b):
    return lax.dot_general(a, b, (((1,), (1,)), ((), ())), preferred_element_type=F32)


def _split2(a):
    hi = a.astype(BF16)
    lo = (a - hi.astype(F32)).astype(BF16)
    return hi, lo


def _dot3(a, b):
    ah, al = _split2(a)
    bh, bl = _split2(b)
    return _dot(ah, bh) + (_dot(ah, bl) + _dot(al, bh))


def _silu(x):
    return x / (1.0 + jnp.exp(-x))


def _mod_body(x_ref, w_ref, b_ref, o_ref):
    o_ref[0] = _dot3(x_ref[...], w_ref[0]) + b_ref[0]


def _modulation(sc8, w_mod, b_mod):
    L, D, N = w_mod.shape
    tn = _pick(N, (512, 256, 128))
    return pl.pallas_call(
        _mod_body,
        grid=(L, N // tn),
        in_specs=[pl.BlockSpec((SUBLANES, D), lambda l, j: (0, 0)),
                  pl.BlockSpec((1, D, tn), lambda l, j: (l, 0, j)),
                  pl.BlockSpec((1, 1, tn), lambda l, j: (l, 0, j))],
        out_specs=pl.BlockSpec((1, SUBLANES, tn), lambda l, j: (l, 0, j)),
        out_shape=jax.ShapeDtypeStruct((L, SUBLANES, N), F32),
        compiler_params=_cparams(("arbitrary", "arbitrary")),
        name="modulation",
    )(sc8, w_mod, b_mod.reshape(L, 1, N))


def _seg_map(tr, T, B):
    return lambda i, *_: (jnp.minimum((i * tr) // T, B), 0, 0)


def _modcast_body(x_ref, m_ref, o_ref):
    m = m_ref[0]
    o_ref[...] = (x_ref[...] * (1.0 + m[1:2, :]) + m[0:1, :]).astype(o_ref.dtype)


def _modcast(xa, mod3, T, B):
    NA, D = xa.shape
    tr = _pick(NA, (256, 128))
    return pl.pallas_call(
        _modcast_body,
        grid=(NA // tr,),
        in_specs=[pl.BlockSpec((tr, D), lambda i: (i, 0)),
                  pl.BlockSpec((1, 6, D), _seg_map(tr, T, B))],
        out_specs=pl.BlockSpec((tr, D), lambda i: (i, 0)),
        out_shape=jax.ShapeDtypeStruct((NA, D), BF16),
        compiler_params=_cparams(("arbitrary",)),
        name="modcast",
    )(xa, mod3)


def _mm_body(n, out_dtype, *refs):
    xs, ws, o_ref = refs[:n], refs[n:2 * n], refs[2 * n]
    acc = _dot(xs[0][...], ws[0][...])
    for x_ref, w_ref in zip(xs[1:], ws[1:]):
        acc = acc + _dot(x_ref[...], w_ref[...])
    o_ref[...] = acc.astype(out_dtype)


def _mm(xs, ws, out_dtype, name):
    M = xs[0].shape[0]
    N = ws[0].shape[1]
    tm = _pick(M, (1536, 1024, 768, 512, 256))
    tn = _pick(N, (512, 256, 128))
    n = len(xs)
    in_specs = ([pl.BlockSpec((tm, x.shape[1]), lambda i, j: (i, 0)) for x in xs]
                + [pl.BlockSpec((w.shape[0], tn), lambda i, j: (0, j)) for w in ws])
    return pl.pallas_call(
        functools.partial(_mm_body, n, out_dtype),
        grid=(M // tm, N // tn),
        in_specs=in_specs,
        out_specs=pl.BlockSpec((tm, tn), lambda i, j: (i, j)),
        out_shape=jax.ShapeDtypeStruct((M, N), out_dtype),
        compiler_params=_cparams(("arbitrary", "arbitrary")),
        name=name,
    )(*xs, *ws)


def _gdn_in_body(B, T, Tc, tr, prev_ref, cur_ref, next_ref, w_ref, o_ref, buf):
    i = pl.program_id(0)
    j = pl.program_id(1)
    row0 = i * tr
    nl = B * T
    is_lat = row0 < nl
    r = jnp.where(is_lat, lax.rem(row0, T), lax.rem(row0 - nl, Tc))
    seq = jnp.where(is_lat, T, Tc)
    at_start = r == 0
    at_end = r + tr == seq
    buf[0:SUBLANES, :] = jnp.where(at_start, 0.0, prev_ref[...])
    buf[SUBLANES:SUBLANES + tr, :] = cur_ref[...]
    buf[SUBLANES + tr:2 * SUBLANES + tr, :] = jnp.where(at_end, 0.0, next_ref[...])
    pad = GDN_CONV_W // 2
    acc = buf[pl.ds(SUBLANES - pad, tr), :] * w_ref[0:1, :]
    for t in range(1, GDN_CONV_W):
        acc = acc + buf[pl.ds(SUBLANES - pad + t, tr), :] * w_ref[t:t + 1, :]
    act = _silu(acc)
    scale = jnp.where(j == 0, HEAD_DIM ** -0.5, 1.0)
    do_norm = j < 2
    for h in range(GDN_HEADS):
        a = act[:, h * HEAD_DIM:(h + 1) * HEAD_DIM]
        nrm = a * lax.rsqrt(jnp.sum(a * a, axis=-1, keepdims=True) + RMS_EPS) * scale
        o_ref[:, h * HEAD_DIM:(h + 1) * HEAD_DIM] = jnp.where(do_norm, nrm, a)


def _gdn_inputs(p_main, conv_w, B, T, Tc):
    NA = p_main.shape[0]
    tr = 256
    assert T % tr == 0 and Tc % tr == 0
    nb8 = NA // SUBLANES
    r8 = tr // SUBLANES
    W = GDN_WIDTH
    return pl.pallas_call(
        functools.partial(_gdn_in_body, B, T, Tc, tr),
        grid=(NA // tr, 3),
        in_specs=[pl.BlockSpec((SUBLANES, W), lambda i, j: (jnp.maximum(i * r8 - 1, 0), j)),
                  pl.BlockSpec((tr, W), lambda i, j: (i, j)),
                  pl.BlockSpec((SUBLANES, W), lambda i, j: (jnp.minimum((i + 1) * r8, nb8 - 1), j)),
                  pl.BlockSpec((GDN_CONV_W, W), lambda i, j: (0, j))],
        out_specs=pl.BlockSpec((tr, W), lambda i, j: (i, j)),
        out_shape=jax.ShapeDtypeStruct((NA, 3 * W), F32),
        scratch_shapes=[pltpu.VMEM((tr + 2 * SUBLANES, W), F32)],
        compiler_params=_cparams(("arbitrary", "arbitrary")),
        name="gdn_inputs",
    )(p_main, p_main, p_main, conv_w)


def _softplus(x):
    return jnp.maximum(x, 0.0) + jnp.log1p(jnp.exp(-jnp.abs(x)))


def _sigmoid(x):
    return 1.0 / (1.0 + jnp.exp(-x))


def _split3(a):
    hi = a.astype(BF16)
    r1 = a - hi.astype(F32)
    mid = r1.astype(BF16)
    lo = (r1 - mid.astype(F32)).astype(BF16)
    return hi, mid, lo


def _gdn_scan_body(q_ref, k_ref, v_ref, ab_ref, abt_ref, prow_ref, pcol_ref, o_ref, s_ref):
    d = pl.program_id(1)
    s = pl.program_id(2)
    C = GDN_CHUNK
    H = GDN_HEADS

    @pl.when(s == 0)
    def _():
        s_ref[...] = jnp.zeros_like(s_ref)

    ii = lax.broadcasted_iota(jnp.int32, (C, C), 0)
    jj = lax.broadcasted_iota(jnp.int32, (C, C), 1)
    ahead = (ii - jj) * jnp.where(d == 0, 1, -1)
    incl = ahead >= 0
    strict = ahead > 0
    tri = jnp.where(incl, 1.0, 0.0).astype(BF16)
    tri_t = jnp.where(ahead <= 0, 1.0, 0.0).astype(BF16)
    eye = jnp.where(ii == jj, 1.0, 0.0)
    bs = GDN_INV_BLOCK
    diag_blk = (ii // bs) == (jj // bs)
    pair_off = []
    while bs < C:
        pair_off.append(((ii // bs) ^ (jj // bs)) == 1)
        bs *= 2

    ab = ab_ref[...]
    abt = abt_ref[0]
    prow = prow_ref[0]
    pcol = pcol_ref[0]
    g = -jnp.exp(prow[0:1, :]) * _softplus(ab + prow[1:2, :])
    gt = -jnp.exp(pcol[:, 0:1]) * _softplus(abt + pcol[:, 1:2])
    beta = _sigmoid(ab)
    g1, g2, g3 = _split3(g)
    gc = _dot(tri, g1) + (_dot(tri, g2) + _dot(tri, g3))
    t1, t2, t3 = _split3(gt)
    gct = _dot(t1, tri_t) + (_dot(t2, tri_t) + _dot(t3, tri_t))
    gtot = jnp.sum(g, axis=0, keepdims=True)

    for h in range(H):
        lo, hi = h * HEAD_DIM, (h + 1) * HEAD_DIM
        q = q_ref[:, lo:hi]
        k = k_ref[:, lo:hi]
        v = v_ref[:, lo:hi]
        bcol = beta[:, H + h:H + h + 1]
        gcol = gc[:, h:h + 1]
        grow = gct[h:h + 1, :]
        glast = gtot[:, h:h + 1]
        decay = jnp.where(incl, jnp.exp(jnp.minimum(gcol - grow, 0.0)), 0.0)
        kb = k.astype(BF16)
        kk = _dot_nt(kb, kb)
        qk = _dot_nt(q.astype(BF16), kb) * decay
        a_mat = jnp.where(strict, kk * decay * bcol, 0.0)
        p = jnp.where(diag_blk, -a_mat, 0.0)
        tm = eye + p
        for _ in range(int(math.log2(GDN_INV_BLOCK)) - 1):
            p = _dot3(p, p)
            tm = tm + _dot3(tm, p)
        for off in pair_off:
            tm = tm - _dot3(tm, _dot3(jnp.where(off, a_mat, 0.0), tm))
        eg = jnp.exp(gcol)
        rhs = jnp.concatenate([v * bcol, k * (bcol * eg)], axis=1).astype(BF16)
        uw = _dot(tm.astype(BF16), rhs)
        u, w = uw[:, :HEAD_DIM], uw[:, HEAD_DIM:]
        st = s_ref[h]
        stb = st.astype(BF16)
        wq = jnp.concatenate([w, q * eg], axis=0).astype(BF16)
        wqs = _dot(wq, stb)
        v_new = u - wqs[:C]
        vnb = v_new.astype(BF16)
        o = wqs[C:] + _dot(qk.astype(BF16), vnb)
        kdec = (k * jnp.exp(glast - gcol)).T.astype(BF16)
        s_ref[h] = st * jnp.exp(glast) + _dot(kdec, vnb)
        o_ref[0, :, lo:hi] = o


def _gdn_scan(qkvn, p_ab, p_abt, prow, pcol, B, T, Tc):
    NA = qkvn.shape[0]
    C = GDN_CHUNK
    nl, nc = T // C, Tc // C
    W = GDN_WIDTH

    def rb(b, d, s):
        ctx_blk = B * nl + b * nc + jnp.where(d == 0, s, nc - 1 - s)
        lat_blk = b * nl + jnp.where(d == 0, s - nc, nl - 1 - (s - nc))
        return jnp.where(s < nc, ctx_blk, lat_blk)

    def col(j):
        return lambda b, d, s: (rb(b, d, s), j)

    return pl.pallas_call(
        _gdn_scan_body,
        grid=(B, 2, nc + nl),
        in_specs=[pl.BlockSpec((C, W), col(0)),
                  pl.BlockSpec((C, W), col(1)),
                  pl.BlockSpec((C, W), col(2)),
                  pl.BlockSpec((C, LANES), lambda b, d, s: (rb(b, d, s), d)),
                  pl.BlockSpec((1, LANES, C), lambda b, d, s: (rb(b, d, s), d, 0)),
                  pl.BlockSpec((1, SUBLANES, LANES), lambda b, d, s: (d, 0, 0)),
                  pl.BlockSpec((1, LANES, SUBLANES), lambda b, d, s: (d, 0, 0))],
        out_specs=pl.BlockSpec((1, C, W), lambda b, d, s: (d, rb(b, d, s), 0)),
        out_shape=jax.ShapeDtypeStruct((2, NA, W), F32),
        scratch_shapes=[pltpu.VMEM((GDN_HEADS, HEAD_DIM, HEAD_DIM), F32)],
        compiler_params=_cparams(("arbitrary", "arbitrary", "arbitrary")),
        name="gdn_scan",
    )(qkvn, qkvn, qkvn, p_ab, p_abt, prow, pcol)


def _gdn_out_body(o_ref, z_ref, w_ref, y_ref):
    o = o_ref[0] + o_ref[1]
    z = z_ref[...]
    w = w_ref[...]
    for h in range(GDN_HEADS):
        lo, hi = h * HEAD_DIM, (h + 1) * HEAD_DIM
        a = o[:, lo:hi]
        n = a * lax.rsqrt(jnp.mean(a * a, axis=-1, keepdims=True) + RMS_EPS) * w
        y_ref[:, lo:hi] = (n * _silu(z[:, lo:hi])).astype(y_ref.dtype)


def _gdn_output(o_dir, p_main, norm_w):
    NA = p_main.shape[0]
    tr = 256
    W = GDN_WIDTH
    return pl.pallas_call(
        _gdn_out_body,
        grid=(NA // tr,),
        in_specs=[pl.BlockSpec((2, tr, W), lambda i: (0, i, 0)),
                  pl.BlockSpec((tr, W), lambda i: (i, OFF_Z // W)),
                  pl.BlockSpec((1, HEAD_DIM), lambda i: (0, 0))],
        out_specs=pl.BlockSpec((tr, W), lambda i: (i, 0)),
        out_shape=jax.ShapeDtypeStruct((NA, W), BF16),
        compiler_params=_cparams(("arbitrary",)),
        name="gdn_output",
    )(o_dir, p_main, norm_w.reshape(1, HEAD_DIM))


def _fn_stage1_body(w_ref, x_ref, z_ref):
    t1 = x_ref.shape[1]
    z = _dot3(w_ref[...], x_ref[0])
    z_ref[0] = z.reshape(2, t1, z.shape[-1])


def _fn_stage1(xf, w1st):
    B, T1, NN = xf.shape
    tn = _pick(NN, (8192, 4096, 1024))
    return pl.pallas_call(
        _fn_stage1_body,
        grid=(B, NN // tn),
        in_specs=[pl.BlockSpec((2 * T1, T1), lambda b, j: (0, 0)),
                  pl.BlockSpec((1, T1, tn), lambda b, j: (b, 0, j))],
        out_specs=pl.BlockSpec((1, 2, T1, tn), lambda b, j: (b, 0, 0, j)),
        out_shape=jax.ShapeDtypeStruct((B, 2, T1, NN), F32),
        compiler_params=_cparams(("arbitrary", "arbitrary")),
        name="fnet_stage1",
    )(w1st, xf)


def _fn_stage2_body(m_ref, z_ref, cs_ref, fw_ref, o_ref):
    z = z_ref[0, :, 0].reshape(2 * LANES, FN_WIDTH)
    hh = _dot3(m_ref[0], z)
    hr, hi = hh[:LANES], hh[LANES:]
    cs = cs_ref[...]
    cols = []
    for g in range(FN_GROUPS):
        lo, up = g * HEAD_DIM, (g + 1) * HEAD_DIM
        cols.append(_dot3(jnp.concatenate([hr[:, lo:up], hi[:, lo:up]], axis=1), cs))
    fr = jnp.concatenate(cols, axis=1)
    o_ref[0] = _dot(fr.astype(BF16), fw_ref[...]).astype(o_ref.dtype)


def _fn_stage2(z5, m2, cs, fnw, same_z):
    B = z5.shape[0]
    T1 = m2.shape[0]
    zmap = (lambda b, k: (b, 0, 0, 0, 0)) if same_z else (lambda b, k: (b, 0, k, 0, 0))
    return pl.pallas_call(
        _fn_stage2_body,
        grid=(B, T1),
        in_specs=[pl.BlockSpec((1, 2 * LANES, 2 * LANES), lambda b, k: (k, 0, 0)),
                  pl.BlockSpec((1, 2, 1, LANES, FN_WIDTH), zmap),
                  pl.BlockSpec((2 * HEAD_DIM, HEAD_DIM), lambda b, k: (0, 0)),
                  pl.BlockSpec((FN_WIDTH, FN_WIDTH), lambda b, k: (0, 0))],
        out_specs=pl.BlockSpec((1, LANES, FN_WIDTH), lambda b, k: (b, 0, k)),
        out_shape=jax.ShapeDtypeStruct((B, LANES, T1 * FN_WIDTH), BF16),
        compiler_params=_cparams(("arbitrary", "arbitrary")),
        name="fnet_stage2",
    )(m2, z5, cs, fnw)


def _phase(num, den):
    ang = (2.0 * math.pi / den) * lax.rem(num, den).astype(F32)
    return jnp.cos(ang), jnp.sin(ang)


def _fn_tables(T, Tc):
    t1 = T // LANES
    a = jnp.arange(t1, dtype=jnp.int32)
    c1, s1 = _phase(a[:, None] * a[None, :], t1)
    w1st = jnp.concatenate([c1, -s1], axis=0)
    k1 = jnp.arange(t1, dtype=jnp.int32)[:, None, None]
    k2 = jnp.arange(LANES, dtype=jnp.int32)[None, :, None]
    t2 = jnp.arange(LANES, dtype=jnp.int32)[None, None, :]
    c, s = _phase(k2 * t2 * t1 + k1 * t2, T)
    sc = (T * HEAD_DIM) ** -0.5
    m2 = jnp.concatenate([jnp.concatenate([c, s], axis=2), jnp.concatenate([-s, c], axis=2)], axis=1) * sc
    t1c = Tc // LANES
    k = (jnp.arange(t1c, dtype=jnp.int32)[:, None, None]
         + t1c * jnp.arange(LANES, dtype=jnp.int32)[None, :, None])
    t = jnp.arange(Tc, dtype=jnp.int32)[None, None, :]
    cc, sc_ = _phase(k * t, Tc)
    m2c = jnp.concatenate([cc, -sc_], axis=1) * (Tc * HEAD_DIM) ** -0.5
    ch = jnp.arange(HEAD_DIM, dtype=jnp.int32)
    c3, s3 = _phase(ch[:, None] * ch[None, :], HEAD_DIM)
    cs = jnp.concatenate([c3, s3], axis=0)
    return w1st, m2, m2c, cs


def _att_prep_body(B, T, tr, x_ref, cs_ref, sn_ref, qw_ref, kw_ref, o_ref):
    i = pl.program_id(0)
    j = pl.program_id(1)
    nh = x_ref.shape[1] // HEAD_DIM

    @pl.when(j == 4)
    def _():
        o_ref[...] = x_ref[...].astype(o_ref.dtype)

    @pl.when(j < 4)
    def _():
        is_lat = i * tr < B * T
        is_q = j < 3
        w = jnp.where(is_q, qw_ref[...], kw_ref[...])
        scale = jnp.where(is_q, HEAD_DIM ** -0.5, 1.0)
        cs = cs_ref[...]
        sn = sn_ref[...]
        for h in range(nh):
            lo, hi = h * HEAD_DIM, (h + 1) * HEAD_DIM
            a = x_ref[:, lo:hi]
            n = a * lax.rsqrt(jnp.mean(a * a, axis=-1, keepdims=True) + RMS_EPS) * w
            rot = n * cs + pltpu.roll(n, HEAD_DIM // 2, 1) * sn
            o_ref[:, lo:hi] = (jnp.where(is_lat, rot, n) * scale).astype(o_ref.dtype)


def _att_prep(p_main, cs_tab, sn_tab, qw, kw, B, T):
    NA = p_main.shape[0]
    tr = 256
    cw = GQA_KV_WIDTH
    base = OFF_Q // cw
    nrt = T // tr
    return pl.pallas_call(
        functools.partial(_att_prep_body, B, T, tr),
        grid=(NA // tr, 5),
        in_specs=[pl.BlockSpec((tr, cw), lambda i, j: (i, base + j)),
                  pl.BlockSpec((tr, HEAD_DIM), lambda i, j: (lax.rem(i, nrt), 0)),
                  pl.BlockSpec((tr, HEAD_DIM), lambda i, j: (lax.rem(i, nrt), 0)),
                  pl.BlockSpec((1, HEAD_DIM), lambda i, j: (0, 0)),
                  pl.BlockSpec((1, HEAD_DIM), lambda i, j: (0, 0))],
        out_specs=pl.BlockSpec((tr, cw), lambda i, j: (i, j)),
        out_shape=jax.ShapeDtypeStruct((NA, 5 * cw), BF16),
        compiler_params=_cparams(("arbitrary", "arbitrary")),
        name="attention_prep",
    )(p_main, cs_tab, sn_tab, qw.reshape(1, HEAD_DIM), kw.reshape(1, HEAD_DIM))


def _flash_body(tk, n_lat, *refs):
    if n_lat:
        q_ref, kc_ref, vc_ref, kl_ref, vl_ref, o_ref, m_sc, l_sc, acc_sc = refs
    else:
        q_ref, kc_ref, vc_ref, o_ref, m_sc, l_sc, acc_sc = refs
    tq = q_ref.shape[0]
    q = q_ref[...]
    qs = jnp.concatenate([q[:, g * HEAD_DIM:(g + 1) * HEAD_DIM] for g in range(GQA_GROUP)], axis=0)
    m_sc[...] = jnp.full_like(m_sc, -jnp.inf)
    l_sc[...] = jnp.zeros_like(l_sc)
    acc_sc[...] = jnp.zeros_like(acc_sc)

    def step(kc, vc):
        s = _dot_nt(qs, kc)
        m_old = m_sc[...]
        m_new = jnp.maximum(m_old, jnp.max(s, axis=-1, keepdims=True))
        p = jnp.exp(s - m_new)
        alpha = jnp.exp(m_old - m_new)
        l_sc[...] = alpha * l_sc[...] + jnp.sum(p, axis=-1, keepdims=True)
        acc_sc[...] = alpha * acc_sc[...] + _dot(p.astype(BF16), vc)
        m_sc[...] = m_new

    step(kc_ref[...], vc_ref[...])
    if n_lat:
        def loop(c, carry):
            off = pl.multiple_of(c * tk, tk)
            step(kl_ref[pl.ds(off, tk), :], vl_ref[pl.ds(off, tk), :])
            return carry
        lax.fori_loop(0, n_lat, loop, 0)
    o = acc_sc[...] / l_sc[...]
    for g in range(GQA_GROUP):
        o_ref[:, g * HEAD_DIM:(g + 1) * HEAD_DIM] = o[g * tq:(g + 1) * tq].astype(o_ref.dtype)


def _flash(qkv_att, B, T, Tc, latent):
    NL = B * T
    qb = GQA_GROUP * HEAD_DIM
    kcol = GQA_WIDTH // HEAD_DIM
    vcol = (GQA_WIDTH + GQA_KV_WIDTH) // HEAD_DIM
    cb = NL // Tc
    if latent:
        tq = _pick(T, (256, 128))
        tk = _pick(T, (512, 256))
        nq = T // tq
        qmap = lambda b, g, i: (b * nq + i, g)
        rows = NL
    else:
        tq, tk, nq = Tc, 0, 1
        qmap = lambda b, g, i: (cb + b, g)
        rows = B * Tc
    in_specs = [pl.BlockSpec((tq, qb), qmap),
                pl.BlockSpec((Tc, HEAD_DIM), lambda b, g, i: (cb + b, kcol + g)),
                pl.BlockSpec((Tc, HEAD_DIM), lambda b, g, i: (cb + b, vcol + g))]
    args = [qkv_att, qkv_att, qkv_att]
    if latent:
        in_specs += [pl.BlockSpec((T, HEAD_DIM), lambda b, g, i: (b, kcol + g)),
                     pl.BlockSpec((T, HEAD_DIM), lambda b, g, i: (b, vcol + g))]
        args += [qkv_att, qkv_att]
    omap = (lambda b, g, i: (b * nq + i, g)) if latent else (lambda b, g, i: (b, g))
    return pl.pallas_call(
        functools.partial(_flash_body, tk, T // tk if latent else 0),
        grid=(B, GQA_KV_HEADS, nq),
        in_specs=in_specs,
        out_specs=pl.BlockSpec((tq, qb), omap),
        out_shape=jax.ShapeDtypeStruct((rows, GQA_WIDTH), BF16),
        scratch_shapes=[pltpu.VMEM((GQA_GROUP * tq, 1), F32),
                        pltpu.VMEM((GQA_GROUP * tq, 1), F32),
                        pltpu.VMEM((GQA_GROUP * tq, HEAD_DIM), F32)],
        compiler_params=_cparams(("arbitrary", "arbitrary", "arbitrary")),
        name="attention_latent" if latent else "attention_context",
    )(*args)


def _layernorm(v, g, b):
    mu = jnp.mean(v, axis=-1, keepdims=True)
    c = v - mu
    var = jnp.mean(c * c, axis=-1, keepdims=True)
    return c * lax.rsqrt(var + LN_EPS) * g + b


def _ln1_body(alpha, y_ref, x_ref, m_ref, g_ref, b_ref, rw_ref, rb_ref, x1_ref, h2_ref, lg_ref):
    m = m_ref[0]
    x1 = _layernorm(alpha * x_ref[...] + m[2:3, :] * y_ref[...], g_ref[...], b_ref[...])
    x1_ref[...] = x1
    h2 = x1 * (1.0 + m[4:5, :]) + m[3:4, :]
    h2_ref[...] = h2
    lg_ref[...] = _dot3(h2, rw_ref[...]) + rb_ref[...]


def _ln1_router(y, xa, mod3, ln_g, ln_b, rw, rb, alpha, T, B):
    NA, D = xa.shape
    tr = _pick(NA, (256, 128))
    row = lambda i: (i, 0)
    fix = lambda i: (0, 0)
    return pl.pallas_call(
        functools.partial(_ln1_body, alpha),
        grid=(NA // tr,),
        in_specs=[pl.BlockSpec((tr, D), row), pl.BlockSpec((tr, D), row),
                  pl.BlockSpec((1, 6, D), _seg_map(tr, T, B)),
                  pl.BlockSpec((1, D), fix), pl.BlockSpec((1, D), fix),
                  pl.BlockSpec((D, LANES), fix), pl.BlockSpec((1, LANES), fix)],
        out_specs=[pl.BlockSpec((tr, D), row), pl.BlockSpec((tr, D), row), pl.BlockSpec((tr, LANES), row)],
        out_shape=[jax.ShapeDtypeStruct((NA, D), F32), jax.ShapeDtypeStruct((NA, D), F32),
                   jax.ShapeDtypeStruct((NA, LANES), F32)],
        compiler_params=_cparams(("arbitrary",)),
        name="ln1_router",
    )(y, xa, mod3, ln_g.reshape(1, D), ln_b.reshape(1, D), rw, rb)


def _row_copy(src_hbm, row, dst, j, sem):
    return pltpu.make_async_copy(src_hbm.at[pl.ds(row, 1)], dst.at[pl.ds(j, 1)], sem)


def _moe_gather_body(tm, src_ref, h_hbm, o_ref, buf, sem):
    base = pl.program_id(0) * tm

    def start(j, c):
        _row_copy(h_hbm, src_ref[base + j], buf, j, sem).start()
        return c

    def wait(j, c):
        _row_copy(h_hbm, src_ref[base + j], buf, j, sem).wait()
        return c

    lax.fori_loop(0, tm, start, 0)
    lax.fori_loop(0, tm, wait, 0)
    o_ref[...] = buf[...].astype(o_ref.dtype)


def _moe_gather(h2, row_src, tm):
    D = h2.shape[1]
    P = row_src.shape[0]
    return pl.pallas_call(
        functools.partial(_moe_gather_body, tm),
        grid_spec=pltpu.PrefetchScalarGridSpec(
            num_scalar_prefetch=1,
            grid=(P // tm,),
            in_specs=[pl.BlockSpec(memory_space=pl.ANY)],
            out_specs=pl.BlockSpec((tm, D), lambda i, src: (i, 0)),
            scratch_shapes=[pltpu.VMEM((tm, D), F32), pltpu.SemaphoreType.DMA(())]),
        out_shape=jax.ShapeDtypeStruct((P, D), BF16),
        compiler_params=_cparams(("arbitrary",)),
        name="moe_gather",
    )(row_src, h2)


def _moe_ffn_body(te_ref, nv_ref, x_ref, w1_ref, w3_ref, w2_ref, o_ref):
    i = pl.program_id(0)

    @pl.when(i < nv_ref[0])
    def _():
        x = x_ref[...]
        a = _silu(_dot(x, w1_ref[0])) * _dot(x, w3_ref[0])
        o_ref[...] = _dot(a.astype(BF16), w2_ref[0])

    @pl.when(i >= nv_ref[0])
    def _():
        o_ref[...] = jnp.zeros_like(o_ref)


def _moe_ffn(xs, tile_expert, nvalid, w1, w3, w2, tm):
    P, D = xs.shape
    F = w1.shape[2]
    return pl.pallas_call(
        _moe_ffn_body,
        grid_spec=pltpu.PrefetchScalarGridSpec(
            num_scalar_prefetch=2,
            grid=(P // tm,),
            in_specs=[pl.BlockSpec((tm, D), lambda i, te, nv: (i, 0)),
                      pl.BlockSpec((1, D, F), lambda i, te, nv: (te[i], 0, 0)),
                      pl.BlockSpec((1, D, F), lambda i, te, nv: (te[i], 0, 0)),
                      pl.BlockSpec((1, F, D), lambda i, te, nv: (te[i], 0, 0))],
            out_specs=pl.BlockSpec((tm, D), lambda i, te, nv: (i, 0))),
        out_shape=jax.ShapeDtypeStruct((P, D), F32),
        compiler_params=_cparams(("arbitrary",)),
        name="moe_ffn",
    )(tile_expert, nvalid, xs, w1, w3, w2)


def _combine_body(alpha, tr, has_next, pos_ref, ys_hbm, x_ref, gt_ref, m_ref, g_ref, b_ref, *rest):
    if has_next:
        mn_ref, x2_ref, hn_ref, ya, yb, sem = rest
    else:
        x2_ref, ya, yb, sem = rest
    base = pl.program_id(0) * tr

    def start(j, c):
        _row_copy(ys_hbm, pos_ref[2 * (base + j)], ya, j, sem).start()
        _row_copy(ys_hbm, pos_ref[2 * (base + j) + 1], yb, j, sem).start()
        return c

    def wait(j, c):
        _row_copy(ys_hbm, pos_ref[2 * (base + j)], ya, j, sem).wait()
        _row_copy(ys_hbm, pos_ref[2 * (base + j) + 1], yb, j, sem).wait()
        return c

    lax.fori_loop(0, tr, start, 0)
    lax.fori_loop(0, tr, wait, 0)
    gt = gt_ref[...]
    y = gt[:, 0:1] * ya[...] + gt[:, 1:2] * yb[...]
    m = m_ref[0]
    x2 = _layernorm(alpha * x_ref[...] + m[5:6, :] * y, g_ref[...], b_ref[...])
    x2_ref[...] = x2
    if has_next:
        mn = mn_ref[0]
        hn_ref[...] = (x2 * (1.0 + mn[1:2, :]) + mn[0:1, :]).astype(hn_ref.dtype)


def _moe_combine(ys, pos, gates, x1, mod3, ln_g, ln_b, mod3_next, alpha, T, B):
    NA, D = x1.shape
    tr = _pick(NA, (256, 128))
    has_next = mod3_next is not None
    row = lambda i, p: (i, 0)
    fix = lambda i, p: (0, 0)
    seg = _seg_map(tr, T, B)
    in_specs = [pl.BlockSpec(memory_space=pl.ANY),
                pl.BlockSpec((tr, D), row), pl.BlockSpec((tr, LANES), row),
                pl.BlockSpec((1, 6, D), seg),
                pl.BlockSpec((1, D), fix), pl.BlockSpec((1, D), fix)]
    args = [pos, ys, x1, gates, mod3, ln_g.reshape(1, D), ln_b.reshape(1, D)]
    out_specs = [pl.BlockSpec((tr, D), row)]
    out_shape = [jax.ShapeDtypeStruct((NA, D), F32)]
    if has_next:
        in_specs.append(pl.BlockSpec((1, 6, D), seg))
        args.append(mod3_next)
        out_specs.append(pl.BlockSpec((tr, D), row))
        out_shape.append(jax.ShapeDtypeStruct((NA, D), BF16))
    return pl.pallas_call(
        functools.partial(_combine_body, alpha, tr, has_next),
        grid_spec=pltpu.PrefetchScalarGridSpec(
            num_scalar_prefetch=1,
            grid=(NA // tr,),
            in_specs=in_specs,
            out_specs=out_specs,
            scratch_shapes=[pltpu.VMEM((tr, D), F32), pltpu.VMEM((tr, D), F32),
                            pltpu.SemaphoreType.DMA(())]),
        out_shape=out_shape,
        compiler_params=_cparams(("arbitrary",)),
        name="moe_combine",
    )(*args)


def _route(logits, bias_free=True):
    lg = logits[:, :MOE_GROUPS]
    le_all = logits[:, MOE_GROUPS:MOE_GROUPS + MOE_EXPERTS]
    n = lg.shape[0]
    grp = jnp.argmax(lg, -1)
    p_grp = jnp.take_along_axis(jax.nn.softmax(lg, -1), grp[:, None], axis=1)
    le = jnp.take_along_axis(le_all.reshape(n, MOE_GROUPS, MOE_EPG), grp[:, None, None], axis=1)[:, 0]
    top_v, top_i = lax.top_k(le, 2)
    w_sel = p_grp * jax.nn.softmax(top_v, -1)
    eid = grp[:, None] * MOE_EPG + top_i
    return eid.astype(jnp.int32), w_sel


def _dispatch(eid, tm):
    n = eid.shape[0]
    flat = eid.reshape(-1)
    na = flat.shape[0]
    order = jnp.argsort(flat, stable=True).astype(jnp.int32)
    sorted_e = flat[order]
    counts = jnp.zeros((MOE_EXPERTS,), jnp.int32).at[flat].add(1)
    starts = jnp.cumsum(counts) - counts
    ptiles = (counts + tm - 1) // tm
    pstart_t = jnp.cumsum(ptiles) - ptiles
    rank = jnp.arange(na, dtype=jnp.int32) - starts[sorted_e]
    ppos = pstart_t[sorted_e] * tm + rank
    nt = na // tm + MOE_EXPERTS
    row_src = jnp.zeros((nt * tm,), jnp.int32).at[ppos].set(order // 2)
    pos = jnp.zeros((na,), jnp.int32).at[order].set(ppos)
    nvalid = jnp.sum(ptiles).astype(jnp.int32)
    tile_ids = jnp.arange(nt, dtype=jnp.int32)
    tile_expert = jnp.sum(tile_ids[:, None] >= (pstart_t + ptiles)[None, :], axis=1).astype(jnp.int32)
    tile_expert = jnp.minimum(tile_expert, MOE_EXPERTS - 1)
    last_e = tile_expert[jnp.maximum(nvalid - 1, 0)]
    tile_expert = jnp.where(tile_ids < nvalid, tile_expert, last_e)
    return row_src, pos, tile_expert, nvalid.reshape(1)


def _rope_tables(T):
    rows = T // GRID_W
    row = jnp.repeat(jnp.arange(rows, dtype=F32), GRID_W)
    col = jnp.tile(jnp.arange(GRID_W, dtype=F32), rows)
    axis_dim = HEAD_DIM // 2
    inv = ROPE_THETA ** (-jnp.arange(0, axis_dim, 2, dtype=F32) / axis_dim)
    ang = jnp.concatenate([row[:, None] * inv, col[:, None] * inv], -1)
    c, s = jnp.cos(ang), jnp.sin(ang)
    return jnp.concatenate([c, c], -1), jnp.concatenate([-s, s], -1)


def kernel(x, c, ctx, c_ctx, w_mod, b_mod, w_in, gdn_conv, gdn_a_log, gdn_dt_bias, gdn_norm, fn_w,
           q_norm, k_norm, w_out, ln1_g, ln1_b, ln2_g, ln2_b, router_g, router_g_b, router_e,
           router_e_b, w1, w3, w2):
    B, T, D = x.shape
    Tc = ctx.shape[1]
    L = w_mod.shape[0]
    NL, NC = B * T, B * Tc
    alpha = (2 * L) ** 0.25
    H = GDN_HEADS

    xa = jnp.concatenate([x.reshape(NL, D), ctx.reshape(NC, D)], axis=0)
    sc = jax.nn.silu(jnp.concatenate([c, c_ctx[None, :]], axis=0))
    sc8 = jnp.zeros((SUBLANES, D), F32).at[:B + 1].set(sc)
    mod = _modulation(sc8, w_mod, b_mod)
    cs_tab, sn_tab = _rope_tables(T)
    w1st, m2, m2c, cs_dft = _fn_tables(T, Tc)
    T1, T1c = T // LANES, Tc // LANES
    tm_moe = 256

    h = None
    for l in range(L):
        last = l == L - 1
        mod3 = mod[l].reshape(SUBLANES, 6, D)
        if h is None:
            h = _modcast(xa, mod3, T, B)
        wl = w_in[l]
        o = np.cumsum([0, 3 * GDN_WIDTH, GDN_WIDTH, 2 * H, 2 * H, FN_WIDTH, GQA_WIDTH, GQA_KV_WIDTH,
                       GQA_KV_WIDTH])
        w_main = jnp.concatenate([wl[:, o[0]:o[2]], wl[:, o[4]:o[8]]], axis=1).astype(BF16)
        wa, wb = wl[:, o[2]:o[3]], wl[:, o[3]:o[4]]
        zpad = jnp.zeros((D, LANES - 2 * H), F32)
        w_ab = jnp.concatenate([wa[:, :H], wb[:, :H], zpad, wa[:, H:], wb[:, H:], zpad], axis=1).astype(BF16)
        p_main = _mm([h], [w_main], F32, "in_proj")
        p_ab = _mm([h], [w_ab], F32, "in_proj_gates")
        qkvn = _gdn_inputs(p_main, gdn_conv[l], B, T, Tc)
        prow = jnp.zeros((2, SUBLANES, LANES), F32)
        prow = prow.at[:, 0, :H].set(gdn_a_log[l]).at[:, 1, :H].set(gdn_dt_bias[l])
        pcol = jnp.swapaxes(prow, 1, 2)
        p_abt = jnp.swapaxes(p_ab.reshape(-1, GDN_CHUNK, 2 * LANES), 1, 2)
        o_dir = _gdn_scan(qkvn, p_ab, p_abt, prow, pcol, B, T, Tc)
        gdn_y = _gdn_output(o_dir, p_main, gdn_norm[l])
        f = p_main[:, OFF_F:OFF_F + FN_WIDTH]
        fnw = fn_w[l].astype(BF16)
        z = _fn_stage1(f[:NL].reshape(B, T1, LANES * FN_WIDTH), w1st)
        fn_l = _fn_stage2(z.reshape(B, 2, T1, LANES, FN_WIDTH), m2, cs_dft, fnw, same_z=False)
        fn_c = _fn_stage2(f[NL:].reshape(B, 2, 1, LANES, FN_WIDTH), m2c, cs_dft, fnw, same_z=True)
        fn_y = jnp.concatenate([fn_l.reshape(NL, FN_WIDTH), fn_c.reshape(NC, FN_WIDTH)], axis=0)
        qkv_att = _att_prep(p_main, cs_tab, sn_tab, q_norm[l], k_norm[l], B, T)
        at_y = jnp.concatenate([_flash(qkv_att, B, T, Tc, True), _flash(qkv_att, B, T, Tc, False)], axis=0)
        wo = w_out[l]
        ws = [wo[:GDN_WIDTH].astype(BF16), wo[GDN_WIDTH:GDN_WIDTH + FN_WIDTH].astype(BF16),
              wo[GDN_WIDTH + FN_WIDTH:].astype(BF16)]
        y = _mm([gdn_y, fn_y, at_y], ws, F32, "out_proj")
        rw = jnp.concatenate([router_g[l], router_e[l],
                              jnp.zeros((D, LANES - MOE_GROUPS - MOE_EXPERTS), F32)], axis=1)
        rb = jnp.concatenate([router_g_b[l], router_e_b[l],
                              jnp.zeros((LANES - MOE_GROUPS - MOE_EXPERTS,), F32)])[None, :]
        x1, h2, logits = _ln1_router(y, xa, mod3, ln1_g[l], ln1_b[l], rw, rb, alpha, T, B)
        eid, w_sel = _route(logits)
        row_src, pos, tile_expert, nvalid = _dispatch(eid, tm_moe)
        xs = _moe_gather(h2, row_src, tm_moe)
        ys = _moe_ffn(xs, tile_expert, nvalid, w1[l].astype(BF16), w3[l].astype(BF16), w2[l].astype(BF16),
                      tm_moe)
        gates = jnp.zeros((NL + NC, LANES), F32).at[:, :2].set(w_sel)
        mod3_next = None if last else mod[l + 1].reshape(SUBLANES, 6, D)
        res = _moe_combine(ys, pos, gates, x1, mod3, ln2_g[l], ln2_b[l], mod3_next, alpha, T, B)
        xa = res[0]
        h = None if last else res[1]
    return xa[:NL].reshape(B, T, D)
```

```python
import functools
import math

import numpy as np
import jax
import jax.numpy as jnp
from jax import lax
from jax.experimental import pallas as pl
from jax.experimental.pallas import tpu as pltpu

F32 = jnp.float32
BF16 = jnp.bfloat16

HEAD_DIM = 128
GDN_HEADS = 12
GDN_WIDTH = GDN_HEADS * HEAD_DIM
GDN_CONV_W = 5
GDN_CHUNK = 64
GDN_INV_BLOCK = 16
FN_GROUPS = 8
FN_WIDTH = FN_GROUPS * HEAD_DIM
GQA_HEADS = 12
GQA_KV_HEADS = 4
GQA_GROUP = GQA_HEADS // GQA_KV_HEADS
GQA_WIDTH = GQA_HEADS * HEAD_DIM
GQA_KV_WIDTH = GQA_KV_HEADS * HEAD_DIM
GRID_W = 64
ROPE_THETA = 10000.0
MOE_GROUPS = 4
MOE_EPG = 8
MOE_EXPERTS = MOE_GROUPS * MOE_EPG
LN_EPS = 1e-5
RMS_EPS = 1e-6
LANES = 128
SUBLANES = 8
VMEM_LIMIT = 56 * 1024 * 1024
DMA_UNROLL = 8

OFF_QKV = 0
OFF_Z = 3 * GDN_WIDTH
OFF_F = OFF_Z + GDN_WIDTH
OFF_Q = OFF_F + FN_WIDTH
OFF_K = OFF_Q + GQA_WIDTH
OFF_V = OFF_K + GQA_KV_WIDTH
N_MAIN = OFF_V + GQA_KV_WIDTH


def _cparams(sem, **kw):
    return pltpu.CompilerParams(dimension_semantics=sem, vmem_limit_bytes=VMEM_LIMIT, **kw)


def _pick(n, cands):
    for c in cands:
        if n % c == 0:
            return c
    raise ValueError(f"no tile for {n} in {cands}")


def _dot(a, b):
    return jnp.dot(a, b, preferred_element_type=F32)


def _dot_nt(a, b):
    return lax.dot_general(a, b, (((1,), (1,)), ((), ())), preferred_element_type=F32)


def _split2(a):
    hi = a.astype(BF16)
    lo = (a - hi.astype(F32)).astype(BF16)
    return hi, lo


def _dot3(a, b):
    ah, al = _split2(a)
    bh, bl = _split2(b)
    return _dot(ah, bh) + (_dot(ah, bl) + _dot(al, bh))


def _silu(x):
    return x / (1.0 + jnp.exp(-x))


def _mod_body(x_ref, w_ref, b_ref, o_ref):
    o_ref[0] = _dot3(x_ref[...], w_ref[0]) + b_ref[0]


def _modulation(sc8, w_mod, b_mod):
    L, D, N = w_mod.shape
    tn = _pick(N, (512, 256, 128))
    return pl.pallas_call(
        _mod_body,
        grid=(L, N // tn),
        in_specs=[pl.BlockSpec((SUBLANES, D), lambda l, j: (0, 0)),
                  pl.BlockSpec((1, D, tn), lambda l, j: (l, 0, j)),
                  pl.BlockSpec((1, 1, tn), lambda l, j: (l, 0, j))],
        out_specs=pl.BlockSpec((1, SUBLANES, tn), lambda l, j: (l, 0, j)),
        out_shape=jax.ShapeDtypeStruct((L, SUBLANES, N), F32),
        compiler_params=_cparams(("arbitrary", "arbitrary")),
        name="modulation",
    )(sc8, w_mod, b_mod.reshape(L, 1, N))


def _seg_map(tr, T, B):
    return lambda i, *_: (jnp.minimum((i * tr) // T, B), 0, 0)


def _modcast_body(x_ref, m_ref, o_ref):
    m = m_ref[0]
    o_ref[...] = (x_ref[...] * (1.0 + m[1:2, :]) + m[0:1, :]).astype(o_ref.dtype)


def _modcast(xa, mod3, T, B):
    NA, D = xa.shape
    tr = _pick(NA, (256, 128))
    return pl.pallas_call(
        _modcast_body,
        grid=(NA // tr,),
        in_specs=[pl.BlockSpec((tr, D), lambda i: (i, 0)),
                  pl.BlockSpec((1, 6, D), _seg_map(tr, T, B))],
        out_specs=pl.BlockSpec((tr, D), lambda i: (i, 0)),
        out_shape=jax.ShapeDtypeStruct((NA, D), BF16),
        compiler_params=_cparams(("arbitrary",)),
        name="modcast",
    )(xa, mod3)


def _mm_body(n, out_dtype, *refs):
    xs, ws, o_ref = refs[:n], refs[n:2 * n], refs[2 * n]
    acc = _dot(xs[0][...], ws[0][...])
    for x_ref, w_ref in zip(xs[1:], ws[1:]):
        acc = acc + _dot(x_ref[...], w_ref[...])
    o_ref[...] = acc.astype(out_dtype)


def _mm(xs, ws, out_dtype, name):
    M = xs[0].shape[0]
    N = ws[0].shape[1]
    tm = _pick(M, (1536, 1024, 768, 512, 256))
    tn = _pick(N, (512, 256, 128))
    n = len(xs)
    in_specs = ([pl.BlockSpec((tm, x.shape[1]), lambda i, j: (i, 0)) for x in xs]
                + [pl.BlockSpec((w.shape[0], tn), lambda i, j: (0, j)) for w in ws])
    return pl.pallas_call(
        functools.partial(_mm_body, n, out_dtype),
        grid=(M // tm, N // tn),
        in_specs=in_specs,
        out_specs=pl.BlockSpec((tm, tn), lambda i, j: (i, j)),
        out_shape=jax.ShapeDtypeStruct((M, N), out_dtype),
        compiler_params=_cparams(("arbitrary", "arbitrary")),
        name=name,
    )(*xs, *ws)


def _gdn_in_body(B, T, Tc, tr, prev_ref, cur_ref, next_ref, w_ref, o_ref, buf):
    i = pl.program_id(0)
    j = pl.program_id(1)
    row0 = i * tr
    nl = B * T
    is_lat = row0 < nl
    r = jnp.where(is_lat, lax.rem(row0, T), lax.rem(row0 - nl, Tc))
    seq = jnp.where(is_lat, T, Tc)
    at_start = r == 0
    at_end = r + tr == seq
    buf[0:SUBLANES, :] = jnp.where(at_start, 0.0, prev_ref[...])
    buf[SUBLANES:SUBLANES + tr, :] = cur_ref[...]
    buf[SUBLANES + tr:2 * SUBLANES + tr, :] = jnp.where(at_end, 0.0, next_ref[...])
    pad = GDN_CONV_W // 2
    acc = buf[pl.ds(SUBLANES - pad, tr), :] * w_ref[0:1, :]
    for t in range(1, GDN_CONV_W):
        acc = acc + buf[pl.ds(SUBLANES - pad + t, tr), :] * w_ref[t:t + 1, :]
    act = _silu(acc)
    scale = jnp.where(j == 0, HEAD_DIM ** -0.5, 1.0)
    do_norm = j < 2
    for h in range(GDN_HEADS):
        a = act[:, h * HEAD_DIM:(h + 1) * HEAD_DIM]
        nrm = a * lax.rsqrt(jnp.sum(a * a, axis=-1, keepdims=True) + RMS_EPS) * scale
        o_ref[:, h * HEAD_DIM:(h + 1) * HEAD_DIM] = jnp.where(do_norm, nrm, a)


def _gdn_inputs(p_main, conv_w, B, T, Tc):
    NA = p_main.shape[0]
    tr = 256
    assert T % tr == 0 and Tc % tr == 0
    nb8 = NA // SUBLANES
    r8 = tr // SUBLANES
    W = GDN_WIDTH
    return pl.pallas_call(
        functools.partial(_gdn_in_body, B, T, Tc, tr),
        grid=(NA // tr, 3),
        in_specs=[pl.BlockSpec((SUBLANES, W), lambda i, j: (jnp.maximum(i * r8 - 1, 0), j)),
                  pl.BlockSpec((tr, W), lambda i, j: (i, j)),
                  pl.BlockSpec((SUBLANES, W), lambda i, j: (jnp.minimum((i + 1) * r8, nb8 - 1), j)),
                  pl.BlockSpec((GDN_CONV_W, W), lambda i, j: (0, j))],
        out_specs=pl.BlockSpec((tr, W), lambda i, j: (i, j)),
        out_shape=jax.ShapeDtypeStruct((NA, 3 * W), F32),
        scratch_shapes=[pltpu.VMEM((tr + 2 * SUBLANES, W), F32)],
        compiler_params=_cparams(("arbitrary", "arbitrary")),
        name="gdn_inputs",
    )(p_main, p_main, p_main, conv_w)


def _softplus(x):
    return jnp.maximum(x, 0.0) + jnp.log1p(jnp.exp(-jnp.abs(x)))


def _sigmoid(x):
    return 1.0 / (1.0 + jnp.exp(-x))


def _split3(a):
    hi = a.astype(BF16)
    r1 = a - hi.astype(F32)
    mid = r1.astype(BF16)
    lo = (r1 - mid.astype(F32)).astype(BF16)
    return hi, mid, lo


def _gdn_chunk_body(q_ref, k_ref, v_ref, ab_ref, abt_ref, prow_ref, pcol_ref,
                    u_ref, wq_ref, qkd_ref, kdt_ref, eg_ref):
    C = GDN_CHUNK
    H = GDN_HEADS
    heads = range(H)
    ii = lax.broadcasted_iota(jnp.int32, (C, C), 0)
    jj = lax.broadcasted_iota(jnp.int32, (C, C), 1)
    eye = jnp.where(ii == jj, 1.0, 0.0)
    bs = GDN_INV_BLOCK
    diag_blk = (ii // bs) == (jj // bs)
    pair_off = []
    while bs < C:
        pair_off.append(((ii // bs) ^ (jj // bs)) == 1)
        bs *= 2

    sl = [slice(h * HEAD_DIM, (h + 1) * HEAD_DIM) for h in heads]
    kb = [k_ref[:, sl[h]].astype(BF16) for h in heads]
    kk = [_dot_nt(kb[h], kb[h]) for h in heads]
    qk = [_dot_nt(q_ref[:, sl[h]].astype(BF16), kb[h]) for h in heads]

    for d in range(2):
        ahead = (ii - jj) if d == 0 else (jj - ii)
        incl = ahead >= 0
        strict = ahead > 0
        tri = jnp.where(incl, 1.0, 0.0).astype(BF16)
        tri_t = jnp.where(ahead <= 0, 1.0, 0.0).astype(BF16)
        ab = ab_ref[:, d * LANES:(d + 1) * LANES]
        abt = abt_ref[0, d * LANES:(d + 1) * LANES, :]
        prow = prow_ref[d]
        pcol = pcol_ref[d]
        g = -jnp.exp(prow[0:1, :]) * _softplus(ab + prow[1:2, :])
        gt = -jnp.exp(pcol[:, 0:1]) * _softplus(abt + pcol[:, 1:2])
        beta = _sigmoid(ab)
        g1, g2, g3 = _split3(g)
        gc = _dot(tri, g1) + (_dot(tri, g2) + _dot(tri, g3))
        t1, t2, t3 = _split3(gt)
        gct = _dot(t1, tri_t) + (_dot(t2, tri_t) + _dot(t3, tri_t))
        gtot = jnp.sum(g, axis=0, keepdims=True)
        eg_ref[d, 0] = jnp.exp(gtot)

        bcol = [beta[:, H + h:H + h + 1] for h in heads]
        gcol = [gc[:, h:h + 1] for h in heads]
        decay = [jnp.where(incl, jnp.exp(jnp.minimum(gcol[h] - gct[h:h + 1, :], 0.0)), 0.0) for h in heads]
        a_mat = [jnp.where(strict, kk[h] * decay[h] * bcol[h], 0.0) for h in heads]
        for h in heads:
            qkd_ref[d, :, h * C:(h + 1) * C] = (qk[h] * decay[h]).astype(qkd_ref.dtype)
        p = [jnp.where(diag_blk, -a_mat[h], 0.0) for h in heads]
        tm = [eye + p[h] for h in heads]
        for _ in range(int(math.log2(GDN_INV_BLOCK)) - 1):
            p = [_dot3(p[h], p[h]) for h in heads]
            tm = [tm[h] + _dot3(tm[h], p[h]) for h in heads]
        for off in pair_off:
            tb = [tm[h].astype(BF16) for h in heads]
            at = [_dot(jnp.where(off, a_mat[h], 0.0).astype(BF16), tb[h]) for h in heads]
            tm = [tm[h] - _dot(tb[h], at[h].astype(BF16)) for h in heads]
        eg = [jnp.exp(gcol[h]) for h in heads]
        k = [k_ref[:, sl[h]] for h in heads]
        rhs = [jnp.concatenate([v_ref[:, sl[h]] * bcol[h], k[h] * (bcol[h] * eg[h])], axis=1).astype(BF16)
               for h in heads]
        uw = [_dot(tm[h].astype(BF16), rhs[h]) for h in heads]
        for h in heads:
            u_ref[d, :, sl[h]] = uw[h][:, :HEAD_DIM]
            wq_ref[d, 0, h, :C, :] = uw[h][:, HEAD_DIM:].astype(wq_ref.dtype)
            wq_ref[d, 0, h, C:, :] = (q_ref[:, sl[h]] * eg[h]).astype(wq_ref.dtype)
            kdec = k[h] * jnp.exp(gtot[:, h:h + 1] - gcol[h])
            kdt_ref[d, 0, h] = kdec.T.astype(kdt_ref.dtype)


def _gdn_chunks(qkvn, p_ab, p_abt, prow, pcol):
    NA = qkvn.shape[0]
    C = GDN_CHUNK
    H = GDN_HEADS
    W = GDN_WIDTH
    n = NA // C
    return pl.pallas_call(
        _gdn_chunk_body,
        grid=(n,),
        in_specs=[pl.BlockSpec((C, W), lambda i: (i, 0)),
                  pl.BlockSpec((C, W), lambda i: (i, 1)),
                  pl.BlockSpec((C, W), lambda i: (i, 2)),
                  pl.BlockSpec((C, 2 * LANES), lambda i: (i, 0)),
                  pl.BlockSpec((1, 2 * LANES, C), lambda i: (i, 0, 0)),
                  pl.BlockSpec((2, SUBLANES, LANES), lambda i: (0, 0, 0)),
                  pl.BlockSpec((2, LANES, SUBLANES), lambda i: (0, 0, 0))],
        out_specs=[pl.BlockSpec((2, C, W), lambda i: (0, i, 0)),
                   pl.BlockSpec((2, 1, H, 2 * C, HEAD_DIM), lambda i: (0, i, 0, 0, 0)),
                   pl.BlockSpec((2, C, H * C), lambda i: (0, i, 0)),
                   pl.BlockSpec((2, 1, H, HEAD_DIM, C), lambda i: (0, i, 0, 0, 0)),
                   pl.BlockSpec((2, 1, 1, LANES), lambda i: (0, i, 0, 0))],
        out_shape=[jax.ShapeDtypeStruct((2, NA, W), F32),
                   jax.ShapeDtypeStruct((2, n, H, 2 * C, HEAD_DIM), BF16),
                   jax.ShapeDtypeStruct((2, NA, H * C), BF16),
                   jax.ShapeDtypeStruct((2, n, H, HEAD_DIM, C), BF16),
                   jax.ShapeDtypeStruct((2, n, 1, LANES), F32)],
        compiler_params=_cparams(("arbitrary",)),
        name="gdn_chunks",
    )(qkvn, qkvn, qkvn, p_ab, p_abt, prow, pcol)


def _gdn_scan_body(*refs):
    C = GDN_CHUNK
    H = GDN_HEADS
    ins, (o0_ref, o1_ref, s_ref) = refs[:10], refs[10:]
    o_refs = (o0_ref, o1_ref)

    @pl.when(pl.program_id(1) == 0)
    def _():
        s_ref[...] = jnp.zeros_like(s_ref)

    dh = [(d, h) for d in range(2) for h in range(H)]
    u_ref, wq_ref, qkd_ref, kdt_ref, eg_ref = [ins[2 * j:2 * j + 2] for j in range(5)]
    st = {x: s_ref[x[0], x[1]] for x in dh}
    wqs = {(d, h): _dot(wq_ref[d][0, 0, h], st[(d, h)].astype(BF16)) for d, h in dh}
    vnb = {}
    for d, h in dh:
        v_new = u_ref[d][0, :, h * HEAD_DIM:(h + 1) * HEAD_DIM] - wqs[(d, h)][:C]
        vnb[(d, h)] = v_new.astype(BF16)
    for d, h in dh:
        o = wqs[(d, h)][C:] + _dot(qkd_ref[d][0, :, h * C:(h + 1) * C], vnb[(d, h)])
        o_refs[d][:, h * HEAD_DIM:(h + 1) * HEAD_DIM] = o
    for d, h in dh:
        e = eg_ref[d][0, 0][:, h:h + 1]
        s_ref[d, h] = st[(d, h)] * e + _dot(kdt_ref[d][0, 0, h], vnb[(d, h)])


def _gdn_scan(u, wq, qkd, kdt, eg, B, T, Tc):
    NA = u.shape[1]
    C = GDN_CHUNK
    H = GDN_HEADS
    nl, nc = T // C, Tc // C
    W = GDN_WIDTH

    def rb(d):
        def f(b, s):
            ctx_blk = B * nl + b * nc + (s if d == 0 else nc - 1 - s)
            lat_blk = b * nl + ((s - nc) if d == 0 else nl - 1 - (s - nc))
            return jnp.where(s < nc, ctx_blk, lat_blk)
        return f

    in_specs, args = [], []
    for arr, blk, imap in (
            (u, (1, C, W), lambda d: (lambda b, s: (d, rb(d)(b, s), 0))),
            (wq, (1, 1, H, 2 * C, HEAD_DIM), lambda d: (lambda b, s: (d, rb(d)(b, s), 0, 0, 0))),
            (qkd, (1, C, H * C), lambda d: (lambda b, s: (d, rb(d)(b, s), 0))),
            (kdt, (1, 1, H, HEAD_DIM, C), lambda d: (lambda b, s: (d, rb(d)(b, s), 0, 0, 0))),
            (eg, (1, 1, 1, LANES), lambda d: (lambda b, s: (d, rb(d)(b, s), 0, 0)))):
        for d in range(2):
            in_specs.append(pl.BlockSpec(blk, imap(d)))
            args.append(arr)
    return pl.pallas_call(
        _gdn_scan_body,
        grid=(B, nc + nl),
        in_specs=in_specs,
        out_specs=[pl.BlockSpec((C, W), lambda b, s: (rb(0)(b, s), 0)),
                   pl.BlockSpec((C, W), lambda b, s: (rb(1)(b, s), 0))],
        out_shape=[jax.ShapeDtypeStruct((NA, W), F32), jax.ShapeDtypeStruct((NA, W), F32)],
        scratch_shapes=[pltpu.VMEM((2, H, HEAD_DIM, HEAD_DIM), F32)],
        compiler_params=_cparams(("arbitrary", "arbitrary")),
        name="gdn_scan",
    )(*args)


def _gdn_out_body(o0_ref, o1_ref, z_ref, w_ref, y_ref):
    o = o0_ref[...] + o1_ref[...]
    z = z_ref[...]
    w = w_ref[...]
    for h in range(GDN_HEADS):
        lo, hi = h * HEAD_DIM, (h + 1) * HEAD_DIM
        a = o[:, lo:hi]
        n = a * lax.rsqrt(jnp.mean(a * a, axis=-1, keepdims=True) + RMS_EPS) * w
        y_ref[:, lo:hi] = (n * _silu(z[:, lo:hi])).astype(y_ref.dtype)


def _gdn_output(o_fwd, o_bwd, p_main, norm_w):
    NA = p_main.shape[0]
    tr = 256
    W = GDN_WIDTH
    return pl.pallas_call(
        _gdn_out_body,
        grid=(NA // tr,),
        in_specs=[pl.BlockSpec((tr, W), lambda i: (i, 0)),
                  pl.BlockSpec((tr, W), lambda i: (i, 0)),
                  pl.BlockSpec((tr, W), lambda i: (i, OFF_Z // W)),
                  pl.BlockSpec((1, HEAD_DIM), lambda i: (0, 0))],
        out_specs=pl.BlockSpec((tr, W), lambda i: (i, 0)),
        out_shape=jax.ShapeDtypeStruct((NA, W), BF16),
        compiler_params=_cparams(("arbitrary",)),
        name="gdn_output",
    )(o_fwd, o_bwd, p_main, norm_w.reshape(1, HEAD_DIM))


def _fn_stage1_body(w_ref, x_ref, z_ref):
    t1 = x_ref.shape[1]
    z = _dot3(w_ref[...], x_ref[0])
    z_ref[0] = z.reshape(2, t1, z.shape[-1])


def _fn_stage1(xf, w1st):
    B, T1, NN = xf.shape
    tn = _pick(NN, (8192, 4096, 1024))
    return pl.pallas_call(
        _fn_stage1_body,
        grid=(B, NN // tn),
        in_specs=[pl.BlockSpec((2 * T1, T1), lambda b, j: (0, 0)),
                  pl.BlockSpec((1, T1, tn), lambda b, j: (b, 0, j))],
        out_specs=pl.BlockSpec((1, 2, T1, tn), lambda b, j: (b, 0, 0, j)),
        out_shape=jax.ShapeDtypeStruct((B, 2, T1, NN), F32),
        compiler_params=_cparams(("arbitrary", "arbitrary")),
        name="fnet_stage1",
    )(w1st, xf)


def _fn_stage2_body(m_ref, z_ref, cs_ref, fw_ref, o_ref):
    z = z_ref[0, :, 0].reshape(2 * LANES, FN_WIDTH)
    hh = _dot3(m_ref[0], z)
    hr, hi = hh[:LANES], hh[LANES:]
    cs = cs_ref[...]
    cols = []
    for g in range(FN_GROUPS):
        lo, up = g * HEAD_DIM, (g + 1) * HEAD_DIM
        cols.append(_dot3(jnp.concatenate([hr[:, lo:up], hi[:, lo:up]], axis=1), cs))
    fr = jnp.concatenate(cols, axis=1)
    o_ref[0] = _dot(fr.astype(BF16), fw_ref[...]).astype(o_ref.dtype)


def _fn_stage2(z5, m2, cs, fnw, same_z):
    B = z5.shape[0]
    T1 = m2.shape[0]
    zmap = (lambda b, k: (b, 0, 0, 0, 0)) if same_z else (lambda b, k: (b, 0, k, 0, 0))
    return pl.pallas_call(
        _fn_stage2_body,
        grid=(B, T1),
        in_specs=[pl.BlockSpec((1, 2 * LANES, 2 * LANES), lambda b, k: (k, 0, 0)),
                  pl.BlockSpec((1, 2, 1, LANES, FN_WIDTH), zmap),
                  pl.BlockSpec((2 * HEAD_DIM, HEAD_DIM), lambda b, k: (0, 0)),
                  pl.BlockSpec((FN_WIDTH, FN_WIDTH), lambda b, k: (0, 0))],
        out_specs=pl.BlockSpec((1, LANES, FN_WIDTH), lambda b, k: (b, 0, k)),
        out_shape=jax.ShapeDtypeStruct((B, LANES, T1 * FN_WIDTH), BF16),
        compiler_params=_cparams(("arbitrary", "arbitrary")),
        name="fnet_stage2",
    )(m2, z5, cs, fnw)


def _phase(num, den):
    ang = (2.0 * math.pi / den) * lax.rem(num, den).astype(F32)
    return jnp.cos(ang), jnp.sin(ang)


def _fn_tables(T, Tc):
    t1 = T // LANES
    a = jnp.arange(t1, dtype=jnp.int32)
    c1, s1 = _phase(a[:, None] * a[None, :], t1)
    w1st = jnp.concatenate([c1, -s1], axis=0)
    k1 = jnp.arange(t1, dtype=jnp.int32)[:, None, None]
    k2 = jnp.arange(LANES, dtype=jnp.int32)[None, :, None]
    t2 = jnp.arange(LANES, dtype=jnp.int32)[None, None, :]
    c, s = _phase(k2 * t2 * t1 + k1 * t2, T)
    sc = (T * HEAD_DIM) ** -0.5
    m2 = jnp.concatenate([jnp.concatenate([c, s], axis=2), jnp.concatenate([-s, c], axis=2)], axis=1) * sc
    t1c = Tc // LANES
    k = (jnp.arange(t1c, dtype=jnp.int32)[:, None, None]
         + t1c * jnp.arange(LANES, dtype=jnp.int32)[None, :, None])
    t = jnp.arange(Tc, dtype=jnp.int32)[None, None, :]
    cc, sc_ = _phase(k * t, Tc)
    m2c = jnp.concatenate([cc, -sc_], axis=1) * (Tc * HEAD_DIM) ** -0.5
    ch = jnp.arange(HEAD_DIM, dtype=jnp.int32)
    c3, s3 = _phase(ch[:, None] * ch[None, :], HEAD_DIM)
    cs = jnp.concatenate([c3, s3], axis=0)
    return w1st, m2, m2c, cs


def _att_prep_body(B, T, tr, x_ref, cs_ref, sn_ref, qw_ref, kw_ref, o_ref, vt_ref):
    i = pl.program_id(0)
    j = pl.program_id(1)
    nh = x_ref.shape[1] // HEAD_DIM

    @pl.when(j == 4)
    def _():
        vt_ref[...] = x_ref[...].T.astype(vt_ref.dtype)

    @pl.when(j < 4)
    def _():
        is_lat = i * tr < B * T
        is_q = j < 3
        w = jnp.where(is_q, qw_ref[...], kw_ref[...])
        scale = jnp.where(is_q, HEAD_DIM ** -0.5 * math.log2(math.e), 1.0)
        cs = cs_ref[...]
        sn = sn_ref[...]
        for h in range(nh):
            lo, hi = h * HEAD_DIM, (h + 1) * HEAD_DIM
            a = x_ref[:, lo:hi]
            n = a * lax.rsqrt(jnp.mean(a * a, axis=-1, keepdims=True) + RMS_EPS) * w
            rot = n * cs + pltpu.roll(n, HEAD_DIM // 2, 1) * sn
            o_ref[:, lo:hi] = (jnp.where(is_lat, rot, n) * scale).astype(o_ref.dtype)


def _att_prep(p_main, cs_tab, sn_tab, qw, kw, B, T):
    NA = p_main.shape[0]
    tr = 256
    cw = GQA_KV_WIDTH
    base = OFF_Q // cw
    nrt = T // tr
    return pl.pallas_call(
        functools.partial(_att_prep_body, B, T, tr),
        grid=(NA // tr, 5),
        in_specs=[pl.BlockSpec((tr, cw), lambda i, j: (i, base + j)),
                  pl.BlockSpec((tr, HEAD_DIM), lambda i, j: (lax.rem(i, nrt), 0)),
                  pl.BlockSpec((tr, HEAD_DIM), lambda i, j: (lax.rem(i, nrt), 0)),
                  pl.BlockSpec((1, HEAD_DIM), lambda i, j: (0, 0)),
                  pl.BlockSpec((1, HEAD_DIM), lambda i, j: (0, 0))],
        out_specs=[pl.BlockSpec((tr, cw), lambda i, j: (i, jnp.minimum(j, 3))),
                   pl.BlockSpec((cw, tr), lambda i, j: (0, i))],
        out_shape=[jax.ShapeDtypeStruct((NA, 4 * cw), BF16), jax.ShapeDtypeStruct((cw, NA), BF16)],
        compiler_params=_cparams(("arbitrary", "arbitrary")),
        name="attention_prep",
    )(p_main, cs_tab, sn_tab, qw.reshape(1, HEAD_DIM), kw.reshape(1, HEAD_DIM))


def _flash_body(tk, n_lat, *refs):
    if n_lat:
        q_ref, kc_ref, vtc_ref, kl_ref, vtl_ref, o_ref, m_sc, l_sc, acc_sc = refs
    else:
        q_ref, kc_ref, vtc_ref, o_ref, m_sc, l_sc, acc_sc = refs
    tq = q_ref.shape[0]
    q = q_ref[...].astype(F32)
    qs = jnp.concatenate([q[:, g * HEAD_DIM:(g + 1) * HEAD_DIM] for g in range(GQA_GROUP)], axis=0)
    qt = qs.T.astype(BF16)
    m_sc[...] = jnp.full_like(m_sc, -jnp.inf)
    l_sc[...] = jnp.zeros_like(l_sc)
    acc_sc[...] = jnp.zeros_like(acc_sc)

    def step(kc, vtc):
        heads = [slice(g * tq, (g + 1) * tq) for g in range(GQA_GROUP)]
        ss = [_dot(kc, qt[:, sl]) for sl in heads]
        for sl, s in zip(heads, ss):
            m_old = m_sc[:, sl]
            m_new = jnp.maximum(m_old, jnp.max(s, axis=0, keepdims=True))
            p = jnp.exp2(s - m_new)
            alpha = jnp.exp2(m_old - m_new)
            l_sc[:, sl] = alpha * l_sc[:, sl] + jnp.sum(p, axis=0, keepdims=True)
            acc_sc[:, sl] = alpha * acc_sc[:, sl] + _dot(vtc, p.astype(BF16))
            m_sc[:, sl] = m_new

    step(kc_ref[...], vtc_ref[...])
    if n_lat:
        def loop(c, carry):
            off = pl.multiple_of(c * tk, tk)
            step(kl_ref[pl.ds(off, tk), :], vtl_ref[:, pl.ds(off, tk)])
            return carry
        lax.fori_loop(0, n_lat, loop, 0, unroll=2)
    o = (acc_sc[...] / l_sc[...]).T
    for g in range(GQA_GROUP):
        o_ref[:, g * HEAD_DIM:(g + 1) * HEAD_DIM] = o[g * tq:(g + 1) * tq].astype(o_ref.dtype)


def _flash(qk_att, vt_att, B, T, Tc, latent):
    NL = B * T
    qb = GQA_GROUP * HEAD_DIM
    kcol = GQA_WIDTH // HEAD_DIM
    cb = NL // Tc
    if latent:
        tq = _pick(T, (256, 128))
        tk = _pick(T, (512, 256))
        nq = T // tq
        qmap = lambda b, g, i: (b * nq + i, g)
        rows = NL
    else:
        tq, tk, nq = Tc, 0, 1
        qmap = lambda b, g, i: (cb + b, g)
        rows = B * Tc
    in_specs = [pl.BlockSpec((tq, qb), qmap),
                pl.BlockSpec((Tc, HEAD_DIM), lambda b, g, i: (cb + b, kcol + g)),
                pl.BlockSpec((HEAD_DIM, Tc), lambda b, g, i: (g, cb + b))]
    args = [qk_att, qk_att, vt_att]
    if latent:
        in_specs += [pl.BlockSpec((T, HEAD_DIM), lambda b, g, i: (b, kcol + g)),
                     pl.BlockSpec((HEAD_DIM, T), lambda b, g, i: (g, b))]
        args += [qk_att, vt_att]
    omap = (lambda b, g, i: (b * nq + i, g)) if latent else (lambda b, g, i: (b, g))
    return pl.pallas_call(
        functools.partial(_flash_body, tk, T // tk if latent else 0),
        grid=(B, GQA_KV_HEADS, nq),
        in_specs=in_specs,
        out_specs=pl.BlockSpec((tq, qb), omap),
        out_shape=jax.ShapeDtypeStruct((rows, GQA_WIDTH), BF16),
        scratch_shapes=[pltpu.VMEM((1, GQA_GROUP * tq), F32),
                        pltpu.VMEM((1, GQA_GROUP * tq), F32),
                        pltpu.VMEM((HEAD_DIM, GQA_GROUP * tq), F32)],
        compiler_params=_cparams(("arbitrary", "arbitrary", "arbitrary")),
        name="attention_latent" if latent else "attention_context",
    )(*args)


def _layernorm(v, g, b):
    mu = jnp.mean(v, axis=-1, keepdims=True)
    c = v - mu
    var = jnp.mean(c * c, axis=-1, keepdims=True)
    return c * lax.rsqrt(var + LN_EPS) * g + b


def _ln1_body(alpha, y_ref, x_ref, m_ref, g_ref, b_ref, rw_ref, rb_ref, x1_ref, h2_ref, lg_ref):
    m = m_ref[0]
    x1 = _layernorm(alpha * x_ref[...] + m[2:3, :] * y_ref[...], g_ref[...], b_ref[...])
    x1_ref[...] = x1
    h2 = x1 * (1.0 + m[4:5, :]) + m[3:4, :]
    h2_ref[...] = h2
    lg_ref[...] = _dot3(h2, rw_ref[...]) + rb_ref[...]


def _ln1_router(y, xa, mod3, ln_g, ln_b, rw, rb, alpha, T, B):
    NA, D = xa.shape
    tr = _pick(NA, (256, 128))
    row = lambda i: (i, 0)
    fix = lambda i: (0, 0)
    return pl.pallas_call(
        functools.partial(_ln1_body, alpha),
        grid=(NA // tr,),
        in_specs=[pl.BlockSpec((tr, D), row), pl.BlockSpec((tr, D), row),
                  pl.BlockSpec((1, 6, D), _seg_map(tr, T, B)),
                  pl.BlockSpec((1, D), fix), pl.BlockSpec((1, D), fix),
                  pl.BlockSpec((D, LANES), fix), pl.BlockSpec((1, LANES), fix)],
        out_specs=[pl.BlockSpec((tr, D), row), pl.BlockSpec((tr, D), row), pl.BlockSpec((tr, LANES), row)],
        out_shape=[jax.ShapeDtypeStruct((NA, D), F32), jax.ShapeDtypeStruct((NA, D), F32),
                   jax.ShapeDtypeStruct((NA, LANES), F32)],
        compiler_params=_cparams(("arbitrary",)),
        name="ln1_router",
    )(y, xa, mod3, ln_g.reshape(1, D), ln_b.reshape(1, D), rw, rb)


def _row_copy(src_hbm, row, dst, j, sem):
    return pltpu.make_async_copy(src_hbm.at[pl.ds(row, 1)], dst.at[pl.ds(j, 1)], sem)


def _moe_gather_body(tm, src_ref, nv_ref, h_hbm, o_ref, buf, sem):
    i = pl.program_id(0)
    nv = nv_ref[0]
    slot = lax.rem(i, 2)

    def issue(tile, sl):
        def f(j, c):
            _row_copy(h_hbm, src_ref[tile * tm + j], buf.at[sl], j, sem.at[sl]).start()
            return c
        lax.fori_loop(0, tm, f, 0, unroll=DMA_UNROLL)

    def drain(tile, sl):
        def f(j, c):
            _row_copy(h_hbm, src_ref[tile * tm + j], buf.at[sl], j, sem.at[sl]).wait()
            return c
        lax.fori_loop(0, tm, f, 0, unroll=DMA_UNROLL)

    @pl.when((i == 0) & (nv > 0))
    def _():
        issue(0, 0)

    @pl.when(i + 1 < nv)
    def _():
        issue(i + 1, 1 - slot)

    @pl.when(i < nv)
    def _():
        drain(i, slot)
        o_ref[...] = buf[slot].astype(o_ref.dtype)

    @pl.when(i >= nv)
    def _():
        o_ref[...] = jnp.zeros_like(o_ref)


def _moe_gather(h2, row_src, nvalid, tm):
    D = h2.shape[1]
    P = row_src.shape[0]
    return pl.pallas_call(
        functools.partial(_moe_gather_body, tm),
        grid_spec=pltpu.PrefetchScalarGridSpec(
            num_scalar_prefetch=2,
            grid=(P // tm,),
            in_specs=[pl.BlockSpec(memory_space=pl.ANY)],
            out_specs=pl.BlockSpec((tm, D), lambda i, src, nv: (i, 0)),
            scratch_shapes=[pltpu.VMEM((2, tm, D), F32), pltpu.SemaphoreType.DMA((2,))]),
        out_shape=jax.ShapeDtypeStruct((P, D), BF16),
        compiler_params=_cparams(("arbitrary",)),
        name="moe_gather",
    )(row_src, nvalid, h2)


def _moe_ffn_body(te_ref, nv_ref, x_ref, w1_ref, w3_ref, w2_ref, o_ref, w1b, w3b, w2b):
    i = pl.program_id(0)
    used = i < nv_ref[0]
    new_expert = (i == 0) | (te_ref[i] != te_ref[jnp.maximum(i - 1, 0)])

    @pl.when(used & new_expert)
    def _():
        w1b[...] = w1_ref[0, 0].astype(BF16)
        w3b[...] = w3_ref[0, 0].astype(BF16)
        w2b[...] = w2_ref[0, 0].astype(BF16)

    @pl.when(used)
    def _():
        x = x_ref[...]
        a = _silu(_dot(x, w1b[...])) * _dot(x, w3b[...])
        o_ref[...] = _dot(a.astype(BF16), w2b[...])

    @pl.when(jnp.logical_not(used))
    def _():
        o_ref[...] = jnp.zeros_like(o_ref)


def _moe_ffn(xs, tile_expert, nvalid, w1, w3, w2, layer, tm):
    P, D = xs.shape
    F = w1.shape[3]
    wmap = lambda i, te, nv: (layer, te[i], 0, 0)
    once = pl.Buffered(1)
    return pl.pallas_call(
        _moe_ffn_body,
        grid_spec=pltpu.PrefetchScalarGridSpec(
            num_scalar_prefetch=2,
            grid=(P // tm,),
            in_specs=[pl.BlockSpec((tm, D), lambda i, te, nv: (i, 0)),
                      pl.BlockSpec((1, 1, D, F), wmap, pipeline_mode=once),
                      pl.BlockSpec((1, 1, D, F), wmap, pipeline_mode=once),
                      pl.BlockSpec((1, 1, F, D), wmap, pipeline_mode=once)],
            out_specs=pl.BlockSpec((tm, D), lambda i, te, nv: (i, 0)),
            scratch_shapes=[pltpu.VMEM((D, F), BF16), pltpu.VMEM((D, F), BF16), pltpu.VMEM((F, D), BF16)]),
        out_shape=jax.ShapeDtypeStruct((P, D), F32),
        compiler_params=_cparams(("arbitrary",)),
        name="moe_ffn",
    )(tile_expert, nvalid, xs, w1, w3, w2)


def _combine_body(alpha, tr, has_next, pos_ref, ys_hbm, x_ref, gt_ref, m_ref, g_ref, b_ref, *rest):
    if has_next:
        mn_ref, x2_ref, hn_ref, ya, yb, sem = rest
    else:
        x2_ref, ya, yb, sem = rest
    i = pl.program_id(0)
    slot = lax.rem(i, 2)

    def copies(tile, sl, j):
        r = 2 * (tile * tr + j)
        return (_row_copy(ys_hbm, pos_ref[r], ya.at[sl], j, sem.at[sl]),
                _row_copy(ys_hbm, pos_ref[r + 1], yb.at[sl], j, sem.at[sl]))

    def issue(tile, sl):
        def f(j, c):
            for cp in copies(tile, sl, j):
                cp.start()
            return c
        lax.fori_loop(0, tr, f, 0, unroll=DMA_UNROLL)

    @pl.when(i == 0)
    def _():
        issue(0, 0)

    @pl.when(i + 1 < pl.num_programs(0))
    def _():
        issue(i + 1, 1 - slot)

    def drain(j, c):
        for cp in copies(i, slot, j):
            cp.wait()
        return c
    lax.fori_loop(0, tr, drain, 0, unroll=DMA_UNROLL)
    gt = gt_ref[...]
    y = gt[:, 0:1] * ya[slot] + gt[:, 1:2] * yb[slot]
    m = m_ref[0]
    x2 = _layernorm(alpha * x_ref[...] + m[5:6, :] * y, g_ref[...], b_ref[...])
    x2_ref[...] = x2
    if has_next:
        mn = mn_ref[0]
        hn_ref[...] = (x2 * (1.0 + mn[1:2, :]) + mn[0:1, :]).astype(hn_ref.dtype)


def _moe_combine(ys, pos, gates, x1, mod3, ln_g, ln_b, mod3_next, alpha, T, B):
    has_next = mod3_next is not None
    D = x1.shape[1]
    NA = x1.shape[0] if has_next else B * T
    tr = _pick(NA, (256, 128))
    row = lambda i, p: (i, 0)
    fix = lambda i, p: (0, 0)
    seg = _seg_map(tr, T, B)
    in_specs = [pl.BlockSpec(memory_space=pl.ANY),
                pl.BlockSpec((tr, D), row), pl.BlockSpec((tr, LANES), row),
                pl.BlockSpec((1, 6, D), seg),
                pl.BlockSpec((1, D), fix), pl.BlockSpec((1, D), fix)]
    args = [pos, ys, x1, gates, mod3, ln_g.reshape(1, D), ln_b.reshape(1, D)]
    out_specs = [pl.BlockSpec((tr, D), row)]
    out_shape = [jax.ShapeDtypeStruct((NA, D), F32)]
    if has_next:
        in_specs.append(pl.BlockSpec((1, 6, D), seg))
        args.append(mod3_next)
        out_specs.append(pl.BlockSpec((tr, D), row))
        out_shape.append(jax.ShapeDtypeStruct((NA, D), BF16))
    return pl.pallas_call(
        functools.partial(_combine_body, alpha, tr, has_next),
        grid_spec=pltpu.PrefetchScalarGridSpec(
            num_scalar_prefetch=1,
            grid=(NA // tr,),
            in_specs=in_specs,
            out_specs=out_specs,
            scratch_shapes=[pltpu.VMEM((2, tr, D), F32), pltpu.VMEM((2, tr, D), F32),
                            pltpu.SemaphoreType.DMA((2,))]),
        out_shape=out_shape,
        compiler_params=_cparams(("arbitrary",)),
        name="moe_combine",
    )(*args)


def _route(logits):
    lg = logits[:, :MOE_GROUPS]
    le_all = logits[:, MOE_GROUPS:MOE_GROUPS + MOE_EXPERTS]
    n = lg.shape[0]
    grp = jnp.argmax(lg, -1)
    p_grp = jnp.take_along_axis(jax.nn.softmax(lg, -1), grp[:, None], axis=1)
    le = jnp.take_along_axis(le_all.reshape(n, MOE_GROUPS, MOE_EPG), grp[:, None, None], axis=1)[:, 0]
    top_v, top_i = lax.top_k(le, 2)
    w_sel = p_grp * jax.nn.softmax(top_v, -1)
    eid = grp[:, None] * MOE_EPG + top_i
    return eid.astype(jnp.int32), w_sel


def _dispatch(eid, tm):
    flat = eid.reshape(-1)
    na = flat.shape[0]
    experts = jnp.arange(MOE_EXPERTS, dtype=jnp.int32)
    order = jnp.argsort(flat, stable=True).astype(jnp.int32)
    inv = jnp.argsort(order).astype(jnp.int32)
    counts = jnp.sum((flat[:, None] == experts[None, :]).astype(jnp.int32), axis=0)
    starts = jnp.cumsum(counts) - counts
    ptiles = (counts + tm - 1) // tm
    pstart_t = jnp.cumsum(ptiles) - ptiles
    pos = (pstart_t[flat] * tm + (inv - starts[flat])).astype(jnp.int32)
    nt = na // tm + MOE_EXPERTS
    nvalid = jnp.sum(ptiles).astype(jnp.int32)
    tile_ids = jnp.arange(nt, dtype=jnp.int32)
    tile_expert = jnp.sum((tile_ids[:, None] >= (pstart_t + ptiles)[None, :]).astype(jnp.int32), axis=1)
    tile_expert = jnp.minimum(tile_expert, MOE_EXPERTS - 1)
    last_e = tile_expert[jnp.maximum(nvalid - 1, 0)]
    tile_expert = jnp.where(tile_ids < nvalid, tile_expert, last_e)
    prow = jnp.arange(nt * tm, dtype=jnp.int32)
    pe = jnp.repeat(tile_expert, tm)
    within = jnp.minimum(prow - pstart_t[pe] * tm, jnp.maximum(counts[pe] - 1, 0))
    row_src = order[jnp.clip(starts[pe] + within, 0, na - 1)] // 2
    return row_src.astype(jnp.int32), pos, tile_expert.astype(jnp.int32), nvalid.reshape(1)


def _rope_tables(T):
    rows = T // GRID_W
    row = jnp.repeat(jnp.arange(rows, dtype=F32), GRID_W)
    col = jnp.tile(jnp.arange(GRID_W, dtype=F32), rows)
    axis_dim = HEAD_DIM // 2
    inv = ROPE_THETA ** (-jnp.arange(0, axis_dim, 2, dtype=F32) / axis_dim)
    ang = jnp.concatenate([row[:, None] * inv, col[:, None] * inv], -1)
    c, s = jnp.cos(ang), jnp.sin(ang)
    return jnp.concatenate([c, c], -1), jnp.concatenate([-s, s], -1)


def kernel(x, c, ctx, c_ctx, w_mod, b_mod, w_in, gdn_conv, gdn_a_log, gdn_dt_bias, gdn_norm, fn_w,
           q_norm, k_norm, w_out, ln1_g, ln1_b, ln2_g, ln2_b, router_g, router_g_b, router_e,
           router_e_b, w1, w3, w2):
    B, T, D = x.shape
    Tc = ctx.shape[1]
    L = w_mod.shape[0]
    NL, NC = B * T, B * Tc
    alpha = (2 * L) ** 0.25
    H = GDN_HEADS

    xa = jnp.concatenate([x.reshape(NL, D), ctx.reshape(NC, D)], axis=0)
    sc = jax.nn.silu(jnp.concatenate([c, c_ctx[None, :]], axis=0))
    sc8 = jnp.pad(sc, ((0, SUBLANES - B - 1), (0, 0)))
    mod = _modulation(sc8, w_mod, b_mod)
    cs_tab, sn_tab = _rope_tables(T)
    w1st, m2, m2c, cs_dft = _fn_tables(T, Tc)
    T1, T1c = T // LANES, Tc // LANES
    tm_moe = 256

    h = None
    for l in range(L):
        last = l == L - 1
        mod3 = mod[l].reshape(SUBLANES, 6, D)
        if h is None:
            h = _modcast(xa, mod3, T, B)
        wl = w_in[l]
        o = np.cumsum([0, 3 * GDN_WIDTH, GDN_WIDTH, 2 * H, 2 * H, FN_WIDTH, GQA_WIDTH, GQA_KV_WIDTH,
                       GQA_KV_WIDTH])
        w_main = jnp.concatenate([wl[:, o[0]:o[2]], wl[:, o[4]:o[8]]], axis=1).astype(BF16)
        wa, wb = wl[:, o[2]:o[3]], wl[:, o[3]:o[4]]
        zpad = jnp.zeros((D, LANES - 2 * H), F32)
        w_ab = jnp.concatenate([wa[:, :H], wb[:, :H], zpad, wa[:, H:], wb[:, H:], zpad], axis=1).astype(BF16)
        p_main = _mm([h], [w_main], F32, "in_proj")
        p_ab = _mm([h], [w_ab], F32, "in_proj_gates")
        qkvn = _gdn_inputs(p_main, gdn_conv[l], B, T, Tc)
        prow = jnp.pad(jnp.stack([gdn_a_log[l], gdn_dt_bias[l]], axis=1),
                       ((0, 0), (0, SUBLANES - 2), (0, LANES - H)))
        pcol = jnp.swapaxes(prow, 1, 2)
        p_abt = jnp.swapaxes(p_ab.reshape(-1, GDN_CHUNK, 2 * LANES), 1, 2)
        o_fwd, o_bwd = _gdn_scan(*_gdn_chunks(qkvn, p_ab, p_abt, prow, pcol), B, T, Tc)
        gdn_y = _gdn_output(o_fwd, o_bwd, p_main, gdn_norm[l])
        f = p_main[:, OFF_F:OFF_F + FN_WIDTH]
        fnw = fn_w[l].astype(BF16)
        z = _fn_stage1(f[:NL].reshape(B, T1, LANES * FN_WIDTH), w1st)
        fn_l = _fn_stage2(z.reshape(B, 2, T1, LANES, FN_WIDTH), m2, cs_dft, fnw, same_z=False)
        fn_c = _fn_stage2(f[NL:].reshape(B, 2, 1, LANES, FN_WIDTH), m2c, cs_dft, fnw, same_z=True)
        fn_y = jnp.concatenate([fn_l.reshape(NL, FN_WIDTH), fn_c.reshape(NC, FN_WIDTH)], axis=0)
        qk_att, vt_att = _att_prep(p_main, cs_tab, sn_tab, q_norm[l], k_norm[l], B, T)
        at_y = jnp.concatenate([_flash(qk_att, vt_att, B, T, Tc, True),
                                _flash(qk_att, vt_att, B, T, Tc, False)], axis=0)
        wo = w_out[l]
        ws = [wo[:GDN_WIDTH].astype(BF16), wo[GDN_WIDTH:GDN_WIDTH + FN_WIDTH].astype(BF16),
              wo[GDN_WIDTH + FN_WIDTH:].astype(BF16)]
        y = _mm([gdn_y, fn_y, at_y], ws, F32, "out_proj")
        rw = jnp.concatenate([router_g[l], router_e[l],
                              jnp.zeros((D, LANES - MOE_GROUPS - MOE_EXPERTS), F32)], axis=1)
        rb = jnp.concatenate([router_g_b[l], router_e_b[l],
                              jnp.zeros((LANES - MOE_GROUPS - MOE_EXPERTS,), F32)])[None, :]
        x1, h2, logits = _ln1_router(y, xa, mod3, ln1_g[l], ln1_b[l], rw, rb, alpha, T, B)
        eid, w_sel = _route(logits)
        row_src, pos, tile_expert, nvalid = _dispatch(eid, tm_moe)
        xs = _moe_gather(h2, row_src, nvalid, tm_moe)
        ys = _moe_ffn(xs, tile_expert, nvalid, w1, w3, w2, l, tm_moe)
        gates = jnp.pad(w_sel, ((0, 0), (0, LANES - 2)))
        mod3_next = None if last else mod[l + 1].reshape(SUBLANES, 6, D)
        res = _moe_combine(ys, pos, gates, x1, mod3, ln2_g[l], ln2_b[l], mod3_next, alpha, T, B)
        xa = res[0]
        h = None if last else res[1]
    return xa.reshape(B, T, D)
```

```python
import functools
import math

import numpy as np
import jax
import jax.numpy as jnp
from jax import lax
from jax.experimental import pallas as pl
from jax.experimental.pallas import tpu as pltpu

F32 = jnp.float32
BF16 = jnp.bfloat16

HEAD_DIM = 128
GDN_HEADS = 12
GDN_WIDTH = GDN_HEADS * HEAD_DIM
GDN_CONV_W = 5
GDN_CHUNK = 64
GDN_INV_BLOCK = 8
FN_GROUPS = 8
FN_WIDTH = FN_GROUPS * HEAD_DIM
GQA_HEADS = 12
GQA_KV_HEADS = 4
GQA_GROUP = GQA_HEADS // GQA_KV_HEADS
GQA_WIDTH = GQA_HEADS * HEAD_DIM
GQA_KV_WIDTH = GQA_KV_HEADS * HEAD_DIM
GRID_W = 64
ROPE_THETA = 10000.0
MOE_GROUPS = 4
MOE_EPG = 8
MOE_EXPERTS = MOE_GROUPS * MOE_EPG
LN_EPS = 1e-5
RMS_EPS = 1e-6
LANES = 128
SUBLANES = 8
VMEM_LIMIT = 56 * 1024 * 1024
DMA_UNROLL = 8
ATT_ROW_BLOCK = 64

OFF_QKV = 0
OFF_Z = 3 * GDN_WIDTH
OFF_F = OFF_Z + GDN_WIDTH
OFF_Q = OFF_F + FN_WIDTH
OFF_K = OFF_Q + GQA_WIDTH
OFF_V = OFF_K + GQA_KV_WIDTH
N_MAIN = OFF_V + GQA_KV_WIDTH


def _cparams(sem, **kw):
    return pltpu.CompilerParams(dimension_semantics=sem, vmem_limit_bytes=VMEM_LIMIT, **kw)


def _pick(n, cands):
    for c in cands:
        if n % c == 0:
            return c
    raise ValueError(f"no tile for {n} in {cands}")


def _dot(a, b):
    return jnp.dot(a, b, preferred_element_type=F32)


def _dot_nt(a, b):
    return lax.dot_general(a, b, (((1,), (1,)), ((), ())), preferred_element_type=F32)


def _split2(a):
    hi = a.astype(BF16)
    lo = (a - hi.astype(F32)).astype(BF16)
    return hi, lo


def _dot3(a, b):
    ah, al = _split2(a)
    bh, bl = _split2(b)
    return _dot(ah, bh) + (_dot(ah, bl) + _dot(al, bh))


def _silu(x):
    return x / (1.0 + jnp.exp(-x))


def _mod_body(x_ref, w_ref, b_ref, o_ref):
    o_ref[0] = _dot3(x_ref[...], w_ref[0]) + b_ref[0]


def _modulation(sc8, w_mod, b_mod):
    L, D, N = w_mod.shape
    tn = _pick(N, (512, 256, 128))
    return pl.pallas_call(
        _mod_body,
        grid=(L, N // tn),
        in_specs=[pl.BlockSpec((SUBLANES, D), lambda l, j: (0, 0)),
                  pl.BlockSpec((1, D, tn), lambda l, j: (l, 0, j)),
                  pl.BlockSpec((1, 1, tn), lambda l, j: (l, 0, j))],
        out_specs=pl.BlockSpec((1, SUBLANES, tn), lambda l, j: (l, 0, j)),
        out_shape=jax.ShapeDtypeStruct((L, SUBLANES, N), F32),
        compiler_params=_cparams(("arbitrary", "arbitrary")),
        name="modulation",
    )(sc8, w_mod, b_mod.reshape(L, 1, N))


def _seg_map(tr, T, B):
    return lambda i, *_: (jnp.minimum((i * tr) // T, B), 0, 0)


def _modcast_body(x_ref, m_ref, o_ref):
    m = m_ref[0]
    o_ref[...] = (x_ref[...] * (1.0 + m[1:2, :]) + m[0:1, :]).astype(o_ref.dtype)


def _modcast(xa, mod3, T, B):
    NA, D = xa.shape
    tr = _pick(NA, (256, 128))
    return pl.pallas_call(
        _modcast_body,
        grid=(NA // tr,),
        in_specs=[pl.BlockSpec((tr, D), lambda i: (i, 0)),
                  pl.BlockSpec((1, 6, D), _seg_map(tr, T, B))],
        out_specs=pl.BlockSpec((tr, D), lambda i: (i, 0)),
        out_shape=jax.ShapeDtypeStruct((NA, D), BF16),
        compiler_params=_cparams(("arbitrary",)),
        name="modcast",
    )(xa, mod3)


def _mm_body(n, out_dtype, *refs):
    xs, ws, o_ref = refs[:n], refs[n:2 * n], refs[2 * n]
    acc = _dot(xs[0][...], ws[0][...])
    for x_ref, w_ref in zip(xs[1:], ws[1:]):
        acc = acc + _dot(x_ref[...], w_ref[...])
    o_ref[...] = acc.astype(out_dtype)


def _mm(xs, ws, out_dtype, name):
    M = xs[0].shape[0]
    N = ws[0].shape[1]
    tm = _pick(M, (1536, 1024, 768, 512, 256))
    tn = _pick(N, (512, 256, 128))
    n = len(xs)
    in_specs = ([pl.BlockSpec((tm, x.shape[1]), lambda i, j: (i, 0)) for x in xs]
                + [pl.BlockSpec((w.shape[0], tn), lambda i, j: (0, j)) for w in ws])
    return pl.pallas_call(
        functools.partial(_mm_body, n, out_dtype),
        grid=(M // tm, N // tn),
        in_specs=in_specs,
        out_specs=pl.BlockSpec((tm, tn), lambda i, j: (i, j)),
        out_shape=jax.ShapeDtypeStruct((M, N), out_dtype),
        compiler_params=_cparams(("arbitrary", "arbitrary")),
        name=name,
    )(*xs, *ws)


def _gdn_in_body(B, T, Tc, tr, prev_ref, cur_ref, next_ref, w_ref, o_ref, buf):
    i = pl.program_id(0)
    j = pl.program_id(1)
    row0 = i * tr
    nl = B * T
    is_lat = row0 < nl
    r = jnp.where(is_lat, lax.rem(row0, T), lax.rem(row0 - nl, Tc))
    seq = jnp.where(is_lat, T, Tc)
    at_start = r == 0
    at_end = r + tr == seq
    buf[0:SUBLANES, :] = jnp.where(at_start, 0.0, prev_ref[...])
    buf[SUBLANES:SUBLANES + tr, :] = cur_ref[...]
    buf[SUBLANES + tr:2 * SUBLANES + tr, :] = jnp.where(at_end, 0.0, next_ref[...])
    pad = GDN_CONV_W // 2
    acc = buf[pl.ds(SUBLANES - pad, tr), :] * w_ref[0:1, :]
    for t in range(1, GDN_CONV_W):
        acc = acc + buf[pl.ds(SUBLANES - pad + t, tr), :] * w_ref[t:t + 1, :]
    act = _silu(acc)
    scale = jnp.where(j == 0, HEAD_DIM ** -0.5, 1.0)
    do_norm = j < 2
    for h in range(GDN_HEADS):
        a = act[:, h * HEAD_DIM:(h + 1) * HEAD_DIM]
        nrm = a * lax.rsqrt(jnp.sum(a * a, axis=-1, keepdims=True) + RMS_EPS) * scale
        o_ref[:, h * HEAD_DIM:(h + 1) * HEAD_DIM] = jnp.where(do_norm, nrm, a)


def _gdn_inputs(p_main, conv_w, B, T, Tc):
    NA = p_main.shape[0]
    tr = 256
    assert T % tr == 0 and Tc % tr == 0
    nb8 = NA // SUBLANES
    r8 = tr // SUBLANES
    W = GDN_WIDTH
    return pl.pallas_call(
        functools.partial(_gdn_in_body, B, T, Tc, tr),
        grid=(NA // tr, 3),
        in_specs=[pl.BlockSpec((SUBLANES, W), lambda i, j: (jnp.maximum(i * r8 - 1, 0), j)),
                  pl.BlockSpec((tr, W), lambda i, j: (i, j)),
                  pl.BlockSpec((SUBLANES, W), lambda i, j: (jnp.minimum((i + 1) * r8, nb8 - 1), j)),
                  pl.BlockSpec((GDN_CONV_W, W), lambda i, j: (0, j))],
        out_specs=pl.BlockSpec((tr, W), lambda i, j: (i, j)),
        out_shape=jax.ShapeDtypeStruct((NA, 3 * W), F32),
        scratch_shapes=[pltpu.VMEM((tr + 2 * SUBLANES, W), F32)],
        compiler_params=_cparams(("arbitrary", "arbitrary")),
        name="gdn_inputs",
    )(p_main, p_main, p_main, conv_w)


def _softplus(x):
    return jnp.maximum(x, 0.0) + jnp.log1p(jnp.exp(-jnp.abs(x)))


def _sigmoid(x):
    return 1.0 / (1.0 + jnp.exp(-x))


def _split3(a):
    hi = a.astype(BF16)
    r1 = a - hi.astype(F32)
    mid = r1.astype(BF16)
    lo = (r1 - mid.astype(F32)).astype(BF16)
    return hi, mid, lo


def _gdn_chunk_body(q_ref, k_ref, v_ref, ab_ref, abt_ref, prow_ref, pcol_ref,
                    u_ref, wq_ref, qkd_ref, kdt_ref, eg_ref):
    C = GDN_CHUNK
    H = GDN_HEADS
    heads = range(H)
    ii = lax.broadcasted_iota(jnp.int32, (C, C), 0)
    jj = lax.broadcasted_iota(jnp.int32, (C, C), 1)
    eye = jnp.where(ii == jj, 1.0, 0.0)
    bs = GDN_INV_BLOCK
    diag_blk = (ii // bs) == (jj // bs)
    pair_off = []
    while bs < C:
        pair_off.append(((ii // bs) ^ (jj // bs)) == 1)
        bs *= 2

    sl = [slice(h * HEAD_DIM, (h + 1) * HEAD_DIM) for h in heads]
    kb = [k_ref[:, sl[h]].astype(BF16) for h in heads]
    kk = [_dot_nt(kb[h], kb[h]) for h in heads]
    qk = [_dot_nt(q_ref[:, sl[h]].astype(BF16), kb[h]) for h in heads]

    dh = [(d, h) for d in range(2) for h in heads]
    incl, strict, beta, gc, gct, gtot = {}, {}, {}, {}, {}, {}
    for d in range(2):
        ahead = (ii - jj) if d == 0 else (jj - ii)
        incl[d] = ahead >= 0
        strict[d] = ahead > 0
        tri = jnp.where(incl[d], 1.0, 0.0).astype(BF16)
        tri_t = jnp.where(ahead <= 0, 1.0, 0.0).astype(BF16)
        ab = ab_ref[:, d * LANES:(d + 1) * LANES]
        abt = abt_ref[0, d * LANES:(d + 1) * LANES, :]
        prow = prow_ref[d]
        pcol = pcol_ref[d]
        g = -jnp.exp(prow[0:1, :]) * _softplus(ab + prow[1:2, :])
        gt = -jnp.exp(pcol[:, 0:1]) * _softplus(abt + pcol[:, 1:2])
        beta[d] = _sigmoid(ab)
        g1, g2, g3 = _split3(g)
        gc[d] = _dot(tri, g1) + (_dot(tri, g2) + _dot(tri, g3))
        t1, t2, t3 = _split3(gt)
        gct[d] = _dot(t1, tri_t) + (_dot(t2, tri_t) + _dot(t3, tri_t))
        gtot[d] = jnp.sum(g, axis=0, keepdims=True)
        eg_ref[d, 0] = jnp.exp(gtot[d])

    bcol = {x: jnp.broadcast_to(beta[x[0]][:, H + x[1]:H + x[1] + 1], (C, HEAD_DIM)) for x in dh}
    gcol = {x: jnp.broadcast_to(gc[x[0]][:, x[1]:x[1] + 1], (C, HEAD_DIM)) for x in dh}
    decay = {(d, h): jnp.where(incl[d], jnp.exp(jnp.minimum(gcol[(d, h)][:, :C] - gct[d][h:h + 1, :], 0.0)), 0.0)
             for d, h in dh}
    a_mat = {(d, h): jnp.where(strict[d], kk[h] * decay[(d, h)] * bcol[(d, h)][:, :C], 0.0) for d, h in dh}
    for d, h in dh:
        qkd_ref[d, :, h * C:(h + 1) * C] = (qk[h] * decay[(d, h)]).astype(qkd_ref.dtype)
    p = {x: jnp.where(diag_blk, -a_mat[x], 0.0) for x in dh}
    tm = {x: eye + p[x] for x in dh}
    for _ in range(int(math.log2(GDN_INV_BLOCK)) - 1):
        pb = {x: p[x].astype(BF16) for x in dh}
        p = {x: _dot(pb[x], pb[x]) for x in dh}
        tm = {x: tm[x] + _dot(tm[x].astype(BF16), p[x].astype(BF16)) for x in dh}
    for off in pair_off:
        tb = {x: tm[x].astype(BF16) for x in dh}
        at = {x: _dot(jnp.where(off, a_mat[x], 0.0).astype(BF16), tb[x]) for x in dh}
        tm = {x: tm[x] - _dot(tb[x], at[x].astype(BF16)) for x in dh}
    eg = {x: jnp.exp(gcol[x]) for x in dh}
    k = [k_ref[:, sl[h]] for h in heads]
    rhs = {(d, h): jnp.concatenate([v_ref[:, sl[h]] * bcol[(d, h)], k[h] * (bcol[(d, h)] * eg[(d, h)])],
                                   axis=1).astype(BF16) for d, h in dh}
    uw = {x: _dot(tm[x].astype(BF16), rhs[x]) for x in dh}
    for d, h in dh:
        x = (d, h)
        u_ref[d, :, sl[h]] = uw[x][:, :HEAD_DIM]
        wq_ref[d, 0, h, :C, :] = uw[x][:, HEAD_DIM:].astype(wq_ref.dtype)
        wq_ref[d, 0, h, C:, :] = (q_ref[:, sl[h]] * eg[x]).astype(wq_ref.dtype)
        kdec = k[h] * jnp.exp(gtot[d][:, h:h + 1] - gcol[x])
        kdt_ref[d, 0, h] = kdec.T.astype(kdt_ref.dtype)


def _gdn_chunks(qkvn, p_ab, p_abt, prow, pcol):
    NA = qkvn.shape[0]
    C = GDN_CHUNK
    H = GDN_HEADS
    W = GDN_WIDTH
    n = NA // C
    return pl.pallas_call(
        _gdn_chunk_body,
        grid=(n,),
        in_specs=[pl.BlockSpec((C, W), lambda i: (i, 0)),
                  pl.BlockSpec((C, W), lambda i: (i, 1)),
                  pl.BlockSpec((C, W), lambda i: (i, 2)),
                  pl.BlockSpec((C, 2 * LANES), lambda i: (i, 0)),
                  pl.BlockSpec((1, 2 * LANES, C), lambda i: (i, 0, 0)),
                  pl.BlockSpec((2, SUBLANES, LANES), lambda i: (0, 0, 0)),
                  pl.BlockSpec((2, LANES, SUBLANES), lambda i: (0, 0, 0))],
        out_specs=[pl.BlockSpec((2, C, W), lambda i: (0, i, 0)),
                   pl.BlockSpec((2, 1, H, 2 * C, HEAD_DIM), lambda i: (0, i, 0, 0, 0)),
                   pl.BlockSpec((2, C, H * C), lambda i: (0, i, 0)),
                   pl.BlockSpec((2, 1, H, HEAD_DIM, C), lambda i: (0, i, 0, 0, 0)),
                   pl.BlockSpec((2, 1, 1, LANES), lambda i: (0, i, 0, 0))],
        out_shape=[jax.ShapeDtypeStruct((2, NA, W), F32),
                   jax.ShapeDtypeStruct((2, n, H, 2 * C, HEAD_DIM), BF16),
                   jax.ShapeDtypeStruct((2, NA, H * C), BF16),
                   jax.ShapeDtypeStruct((2, n, H, HEAD_DIM, C), BF16),
                   jax.ShapeDtypeStruct((2, n, 1, LANES), F32)],
        compiler_params=_cparams(("arbitrary",)),
        name="gdn_chunks",
    )(qkvn, qkvn, qkvn, p_ab, p_abt, prow, pcol)


def _gdn_scan_body(*refs):
    C = GDN_CHUNK
    H = GDN_HEADS
    ins, (o0_ref, o1_ref, s_ref) = refs[:10], refs[10:]
    o_refs = (o0_ref, o1_ref)

    @pl.when(pl.program_id(1) == 0)
    def _():
        s_ref[...] = jnp.zeros_like(s_ref)

    dh = [(d, h) for d in range(2) for h in range(H)]
    u_ref, wq_ref, qkd_ref, kdt_ref, eg_ref = [ins[2 * j:2 * j + 2] for j in range(5)]
    st = {x: s_ref[x[0], x[1]] for x in dh}
    wqs = {(d, h): _dot(wq_ref[d][0, 0, h], st[(d, h)].astype(BF16)) for d, h in dh}
    vnb = {}
    for d, h in dh:
        v_new = u_ref[d][0, :, h * HEAD_DIM:(h + 1) * HEAD_DIM] - wqs[(d, h)][:C]
        vnb[(d, h)] = v_new.astype(BF16)
    for d, h in dh:
        o = wqs[(d, h)][C:] + _dot(qkd_ref[d][0, :, h * C:(h + 1) * C], vnb[(d, h)])
        o_refs[d][:, h * HEAD_DIM:(h + 1) * HEAD_DIM] = o
    for d, h in dh:
        e = eg_ref[d][0, 0][:, h:h + 1]
        s_ref[d, h] = st[(d, h)] * e + _dot(kdt_ref[d][0, 0, h], vnb[(d, h)])


def _gdn_scan(u, wq, qkd, kdt, eg, B, T, Tc):
    NA = u.shape[1]
    C = GDN_CHUNK
    H = GDN_HEADS
    nl, nc = T // C, Tc // C
    W = GDN_WIDTH

    def rb(d):
        def f(b, s):
            ctx_blk = B * nl + b * nc + (s if d == 0 else nc - 1 - s)
            lat_blk = b * nl + ((s - nc) if d == 0 else nl - 1 - (s - nc))
            return jnp.where(s < nc, ctx_blk, lat_blk)
        return f

    in_specs, args = [], []
    for arr, blk, imap in (
            (u, (1, C, W), lambda d: (lambda b, s: (d, rb(d)(b, s), 0))),
            (wq, (1, 1, H, 2 * C, HEAD_DIM), lambda d: (lambda b, s: (d, rb(d)(b, s), 0, 0, 0))),
            (qkd, (1, C, H * C), lambda d: (lambda b, s: (d, rb(d)(b, s), 0))),
            (kdt, (1, 1, H, HEAD_DIM, C), lambda d: (lambda b, s: (d, rb(d)(b, s), 0, 0, 0))),
            (eg, (1, 1, 1, LANES), lambda d: (lambda b, s: (d, rb(d)(b, s), 0, 0)))):
        for d in range(2):
            in_specs.append(pl.BlockSpec(blk, imap(d)))
            args.append(arr)
    return pl.pallas_call(
        _gdn_scan_body,
        grid=(B, nc + nl),
        in_specs=in_specs,
        out_specs=[pl.BlockSpec((C, W), lambda b, s: (rb(0)(b, s), 0)),
                   pl.BlockSpec((C, W), lambda b, s: (rb(1)(b, s), 0))],
        out_shape=[jax.ShapeDtypeStruct((NA, W), F32), jax.ShapeDtypeStruct((NA, W), F32)],
        scratch_shapes=[pltpu.VMEM((2, H, HEAD_DIM, HEAD_DIM), F32)],
        compiler_params=_cparams(("arbitrary", "arbitrary")),
        name="gdn_scan",
    )(*args)


def _gdn_out_body(o0_ref, o1_ref, z_ref, w_ref, y_ref):
    o = o0_ref[...] + o1_ref[...]
    z = z_ref[...]
    w = w_ref[...]
    for h in range(GDN_HEADS):
        lo, hi = h * HEAD_DIM, (h + 1) * HEAD_DIM
        a = o[:, lo:hi]
        n = a * lax.rsqrt(jnp.mean(a * a, axis=-1, keepdims=True) + RMS_EPS) * w
        y_ref[:, lo:hi] = (n * _silu(z[:, lo:hi])).astype(y_ref.dtype)


def _gdn_output(o_fwd, o_bwd, p_main, norm_w):
    NA = p_main.shape[0]
    tr = 256
    W = GDN_WIDTH
    return pl.pallas_call(
        _gdn_out_body,
        grid=(NA // tr,),
        in_specs=[pl.BlockSpec((tr, W), lambda i: (i, 0)),
                  pl.BlockSpec((tr, W), lambda i: (i, 0)),
                  pl.BlockSpec((tr, W), lambda i: (i, OFF_Z // W)),
                  pl.BlockSpec((1, HEAD_DIM), lambda i: (0, 0))],
        out_specs=pl.BlockSpec((tr, W), lambda i: (i, 0)),
        out_shape=jax.ShapeDtypeStruct((NA, W), BF16),
        compiler_params=_cparams(("arbitrary",)),
        name="gdn_output",
    )(o_fwd, o_bwd, p_main, norm_w.reshape(1, HEAD_DIM))


def _fn_stage1_body(w_ref, x_ref, z_ref):
    t1 = x_ref.shape[1]
    z = _dot3(w_ref[...], x_ref[0])
    z_ref[0] = z.reshape(2, t1, z.shape[-1])


def _fn_stage1(xf, w1st):
    B, T1, NN = xf.shape
    tn = _pick(NN, (8192, 4096, 1024))
    return pl.pallas_call(
        _fn_stage1_body,
        grid=(B, NN // tn),
        in_specs=[pl.BlockSpec((2 * T1, T1), lambda b, j: (0, 0)),
                  pl.BlockSpec((1, T1, tn), lambda b, j: (b, 0, j))],
        out_specs=pl.BlockSpec((1, 2, T1, tn), lambda b, j: (b, 0, 0, j)),
        out_shape=jax.ShapeDtypeStruct((B, 2, T1, NN), F32),
        compiler_params=_cparams(("arbitrary", "arbitrary")),
        name="fnet_stage1",
    )(w1st, xf)


def _fn_stage2_body(m_ref, z_ref, cs_ref, fw_ref, o_ref):
    z = z_ref[0, :, 0].reshape(2 * LANES, FN_WIDTH)
    hh = _dot3(m_ref[0], z)
    hr, hi = hh[:LANES], hh[LANES:]
    cs = cs_ref[...]
    cols = []
    for g in range(FN_GROUPS):
        lo, up = g * HEAD_DIM, (g + 1) * HEAD_DIM
        cols.append(_dot3(jnp.concatenate([hr[:, lo:up], hi[:, lo:up]], axis=1), cs))
    fr = jnp.concatenate(cols, axis=1)
    o_ref[0] = _dot(fr.astype(BF16), fw_ref[...]).astype(o_ref.dtype)


def _fn_stage2(z5, m2, cs, fnw, same_z):
    B = z5.shape[0]
    T1 = m2.shape[0]
    zmap = (lambda b, k: (b, 0, 0, 0, 0)) if same_z else (lambda b, k: (b, 0, k, 0, 0))
    return pl.pallas_call(
        _fn_stage2_body,
        grid=(B, T1),
        in_specs=[pl.BlockSpec((1, 2 * LANES, 2 * LANES), lambda b, k: (k, 0, 0)),
                  pl.BlockSpec((1, 2, 1, LANES, FN_WIDTH), zmap),
                  pl.BlockSpec((2 * HEAD_DIM, HEAD_DIM), lambda b, k: (0, 0)),
                  pl.BlockSpec((FN_WIDTH, FN_WIDTH), lambda b, k: (0, 0))],
        out_specs=pl.BlockSpec((1, LANES, FN_WIDTH), lambda b, k: (b, 0, k)),
        out_shape=jax.ShapeDtypeStruct((B, LANES, T1 * FN_WIDTH), BF16),
        compiler_params=_cparams(("arbitrary", "arbitrary")),
        name="fnet_stage2",
    )(m2, z5, cs, fnw)


def _phase(num, den):
    ang = (2.0 * math.pi / den) * lax.rem(num, den).astype(F32)
    return jnp.cos(ang), jnp.sin(ang)


def _fn_tables(T, Tc):
    t1 = T // LANES
    a = jnp.arange(t1, dtype=jnp.int32)
    c1, s1 = _phase(a[:, None] * a[None, :], t1)
    w1st = jnp.concatenate([c1, -s1], axis=0)
    k1 = jnp.arange(t1, dtype=jnp.int32)[:, None, None]
    k2 = jnp.arange(LANES, dtype=jnp.int32)[None, :, None]
    t2 = jnp.arange(LANES, dtype=jnp.int32)[None, None, :]
    c, s = _phase(k2 * t2 * t1 + k1 * t2, T)
    sc = (T * HEAD_DIM) ** -0.5
    m2 = jnp.concatenate([jnp.concatenate([c, s], axis=2), jnp.concatenate([-s, c], axis=2)], axis=1) * sc
    t1c = Tc // LANES
    k = (jnp.arange(t1c, dtype=jnp.int32)[:, None, None]
         + t1c * jnp.arange(LANES, dtype=jnp.int32)[None, :, None])
    t = jnp.arange(Tc, dtype=jnp.int32)[None, None, :]
    cc, sc_ = _phase(k * t, Tc)
    m2c = jnp.concatenate([cc, -sc_], axis=1) * (Tc * HEAD_DIM) ** -0.5
    ch = jnp.arange(HEAD_DIM, dtype=jnp.int32)
    c3, s3 = _phase(ch[:, None] * ch[None, :], HEAD_DIM)
    cs = jnp.concatenate([c3, s3], axis=0)
    return w1st, m2, m2c, cs


def _att_prep_body(B, T, tr, x_ref, cs_ref, sn_ref, qw_ref, kw_ref, o_ref, vt_ref):
    i = pl.program_id(0)
    j = pl.program_id(1)
    nh = x_ref.shape[1] // HEAD_DIM

    @pl.when(j == 4)
    def _():
        vt_ref[...] = x_ref[...].T.astype(vt_ref.dtype)

    @pl.when(j < 4)
    def _():
        is_lat = i * tr < B * T
        is_q = j < 3
        w = jnp.where(is_q, qw_ref[...], kw_ref[...])
        scale = jnp.where(is_q, HEAD_DIM ** -0.5 * math.log2(math.e), 1.0)
        cs = cs_ref[...]
        sn = sn_ref[...]
        for h in range(nh):
            lo, hi = h * HEAD_DIM, (h + 1) * HEAD_DIM
            a = x_ref[:, lo:hi]
            n = a * lax.rsqrt(jnp.mean(a * a, axis=-1, keepdims=True) + RMS_EPS) * w
            rot = n * cs + pltpu.roll(n, HEAD_DIM // 2, 1) * sn
            o_ref[:, lo:hi] = (jnp.where(is_lat, rot, n) * scale).astype(o_ref.dtype)


def _att_prep(p_main, cs_tab, sn_tab, qw, kw, B, T):
    NA = p_main.shape[0]
    tr = 256
    cw = GQA_KV_WIDTH
    base = OFF_Q // cw
    nrt = T // tr
    return pl.pallas_call(
        functools.partial(_att_prep_body, B, T, tr),
        grid=(NA // tr, 5),
        in_specs=[pl.BlockSpec((tr, cw), lambda i, j: (i, base + j)),
                  pl.BlockSpec((tr, HEAD_DIM), lambda i, j: (lax.rem(i, nrt), 0)),
                  pl.BlockSpec((tr, HEAD_DIM), lambda i, j: (lax.rem(i, nrt), 0)),
                  pl.BlockSpec((1, HEAD_DIM), lambda i, j: (0, 0)),
                  pl.BlockSpec((1, HEAD_DIM), lambda i, j: (0, 0))],
        out_specs=[pl.BlockSpec((tr, cw), lambda i, j: (i, jnp.minimum(j, 3))),
                   pl.BlockSpec((cw, tr), lambda i, j: (0, i))],
        out_shape=[jax.ShapeDtypeStruct((NA, 4 * cw), BF16), jax.ShapeDtypeStruct((cw, NA), BF16)],
        compiler_params=_cparams(("arbitrary", "arbitrary")),
        name="attention_prep",
    )(p_main, cs_tab, sn_tab, qw.reshape(1, HEAD_DIM), kw.reshape(1, HEAD_DIM))


def _flash_body(tk, n_lat, *refs):
    if n_lat:
        q_ref, kc_ref, vtc_ref, kl_ref, vtl_ref, o_ref, qt_sc, s_sc, p_sc, m_sc, l_sc, acc_sc = refs
    else:
        q_ref, kc_ref, vtc_ref, o_ref, qt_sc, s_sc, p_sc, m_sc, l_sc, acc_sc = refs
    heads = range(GQA_GROUP)
    tq = q_ref.shape[0]
    for g in heads:
        qt_sc[g] = q_ref[:, g * HEAD_DIM:(g + 1) * HEAD_DIM].astype(F32).T.astype(BF16)
    m_sc[...] = jnp.full_like(m_sc, -jnp.inf)
    l_sc[...] = jnp.zeros_like(l_sc)
    acc_sc[...] = jnp.zeros_like(acc_sc)

    def scores(slot, kc):
        for g in heads:
            s_sc[slot, g, :kc.shape[0], :] = _dot(kc, qt_sc[g])

    def softmax_pv(slot, vtc):
        nk = vtc.shape[1]
        blocks = [slice(r, r + ATT_ROW_BLOCK) for r in range(0, nk, ATT_ROW_BLOCK)]
        alphas = []
        for g in heads:
            cols = slice(g * tq, (g + 1) * tq)
            mx = s_sc[slot, g, blocks[0], :]
            for rb in blocks[1:]:
                mx = jnp.maximum(mx, s_sc[slot, g, rb, :])
            m_old = m_sc[g]
            m_new = jnp.maximum(m_old, jnp.max(mx, axis=0, keepdims=True))
            alpha = jnp.exp2(m_old - m_new)
            psum = None
            for rb in blocks:
                p = jnp.exp2(s_sc[slot, g, rb, :] - m_new)
                psum = p if psum is None else psum + p
                p_sc[slot, rb, cols] = p.astype(BF16)
            l_sc[g] = alpha * l_sc[g] + jnp.sum(psum, axis=0, keepdims=True)
            m_sc[g] = m_new
            alphas.append(alpha)
        acc_sc[...] = jnp.concatenate(alphas, axis=1) * acc_sc[...] + _dot(vtc, p_sc[slot, :nk, :])

    scores(0, kc_ref[...])
    softmax_pv(0, vtc_ref[...])
    if n_lat:
        def pair(c, carry):
            offs = [pl.multiple_of((2 * c + j) * tk, tk) for j in range(2)]
            for j in range(2):
                scores(j, kl_ref[pl.ds(offs[j], tk), :])
            for j in range(2):
                softmax_pv(j, vtl_ref[:, pl.ds(offs[j], tk)])
            return carry
        lax.fori_loop(0, n_lat // 2, pair, 0)
        if n_lat % 2:
            last = (n_lat - 1) * tk
            scores(0, kl_ref[last:last + tk, :])
            softmax_pv(0, vtl_ref[:, last:last + tk])
    for g in heads:
        o = acc_sc[:, g * tq:(g + 1) * tq] / l_sc[g]
        o_ref[:, g * HEAD_DIM:(g + 1) * HEAD_DIM] = o.T.astype(o_ref.dtype)


def _flash(qk_att, vt_att, B, T, Tc, latent):
    NL = B * T
    qb = GQA_GROUP * HEAD_DIM
    kcol = GQA_WIDTH // HEAD_DIM
    cb = NL // Tc
    if latent:
        tq = _pick(T, (256, 128))
        tk = _pick(T, (512, 256))
        nq = T // tq
        qmap = lambda b, g, i: (b * nq + i, g)
        rows = NL
    else:
        tq, tk, nq = Tc, 0, 1
        qmap = lambda b, g, i: (cb + b, g)
        rows = B * Tc
    in_specs = [pl.BlockSpec((tq, qb), qmap),
                pl.BlockSpec((Tc, HEAD_DIM), lambda b, g, i: (cb + b, kcol + g)),
                pl.BlockSpec((HEAD_DIM, Tc), lambda b, g, i: (g, cb + b))]
    args = [qk_att, qk_att, vt_att]
    if latent:
        in_specs += [pl.BlockSpec((T, HEAD_DIM), lambda b, g, i: (b, kcol + g)),
                     pl.BlockSpec((HEAD_DIM, T), lambda b, g, i: (g, b))]
        args += [qk_att, vt_att]
    omap = (lambda b, g, i: (b * nq + i, g)) if latent else (lambda b, g, i: (b, g))
    return pl.pallas_call(
        functools.partial(_flash_body, tk, T // tk if latent else 0),
        grid=(B, GQA_KV_HEADS, nq),
        in_specs=in_specs,
        out_specs=pl.BlockSpec((tq, qb), omap),
        out_shape=jax.ShapeDtypeStruct((rows, GQA_WIDTH), BF16),
        scratch_shapes=[pltpu.VMEM((GQA_GROUP, HEAD_DIM, tq), BF16),
                        pltpu.VMEM((2, GQA_GROUP, max(tk, Tc), tq), F32),
                        pltpu.VMEM((2, max(tk, Tc), GQA_GROUP * tq), BF16),
                        pltpu.VMEM((GQA_GROUP, 1, tq), F32),
                        pltpu.VMEM((GQA_GROUP, 1, tq), F32),
                        pltpu.VMEM((HEAD_DIM, GQA_GROUP * tq), F32)],
        compiler_params=_cparams(("arbitrary", "arbitrary", "arbitrary")),
        name="attention_latent" if latent else "attention_context",
    )(*args)


def _layernorm(v, g, b):
    mu = jnp.mean(v, axis=-1, keepdims=True)
    c = v - mu
    var = jnp.mean(c * c, axis=-1, keepdims=True)
    return c * lax.rsqrt(var + LN_EPS) * g + b


def _ln1_body(alpha, y_ref, x_ref, m_ref, g_ref, b_ref, rw_ref, rb_ref, x1_ref, h2_ref, lg_ref):
    m = m_ref[0]
    x1 = _layernorm(alpha * x_ref[...] + m[2:3, :] * y_ref[...], g_ref[...], b_ref[...])
    x1_ref[...] = x1
    h2 = x1 * (1.0 + m[4:5, :]) + m[3:4, :]
    h2_ref[...] = h2
    lg_ref[...] = _dot3(h2, rw_ref[...]) + rb_ref[...]


def _ln1_router(y, xa, mod3, ln_g, ln_b, rw, rb, alpha, T, B):
    NA, D = xa.shape
    tr = _pick(NA, (256, 128))
    row = lambda i: (i, 0)
    fix = lambda i: (0, 0)
    return pl.pallas_call(
        functools.partial(_ln1_body, alpha),
        grid=(NA // tr,),
        in_specs=[pl.BlockSpec((tr, D), row), pl.BlockSpec((tr, D), row),
                  pl.BlockSpec((1, 6, D), _seg_map(tr, T, B)),
                  pl.BlockSpec((1, D), fix), pl.BlockSpec((1, D), fix),
                  pl.BlockSpec((D, LANES), fix), pl.BlockSpec((1, LANES), fix)],
        out_specs=[pl.BlockSpec((tr, D), row), pl.BlockSpec((tr, D), row), pl.BlockSpec((tr, LANES), row)],
        out_shape=[jax.ShapeDtypeStruct((NA, D), F32), jax.ShapeDtypeStruct((NA, D), F32),
                   jax.ShapeDtypeStruct((NA, LANES), F32)],
        compiler_params=_cparams(("arbitrary",)),
        name="ln1_router",
    )(y, xa, mod3, ln_g.reshape(1, D), ln_b.reshape(1, D), rw, rb)


def _row_copy(src_hbm, row, dst, j, sem):
    return pltpu.make_async_copy(src_hbm.at[pl.ds(row, 1)], dst.at[pl.ds(j, 1)], sem)


def _moe_gather_body(tm, src_ref, nv_ref, h_hbm, o_ref, buf, sem):
    i = pl.program_id(0)
    nv = nv_ref[0]
    slot = lax.rem(i, 2)

    def issue(tile, sl):
        def f(j, c):
            _row_copy(h_hbm, src_ref[tile * tm + j], buf.at[sl], j, sem.at[sl]).start()
            return c
        lax.fori_loop(0, tm, f, 0, unroll=DMA_UNROLL)

    def drain(tile, sl):
        def f(j, c):
            _row_copy(h_hbm, src_ref[tile * tm + j], buf.at[sl], j, sem.at[sl]).wait()
            return c
        lax.fori_loop(0, tm, f, 0, unroll=DMA_UNROLL)

    @pl.when((i == 0) & (nv > 0))
    def _():
        issue(0, 0)

    @pl.when(i + 1 < nv)
    def _():
        issue(i + 1, 1 - slot)

    @pl.when(i < nv)
    def _():
        drain(i, slot)
        o_ref[...] = buf[slot].astype(o_ref.dtype)

    @pl.when(i >= nv)
    def _():
        o_ref[...] = jnp.zeros_like(o_ref)


def _moe_gather(h2, row_src, nvalid, tm):
    D = h2.shape[1]
    P = row_src.shape[0]
    return pl.pallas_call(
        functools.partial(_moe_gather_body, tm),
        grid_spec=pltpu.PrefetchScalarGridSpec(
            num_scalar_prefetch=2,
            grid=(P // tm,),
            in_specs=[pl.BlockSpec(memory_space=pl.ANY)],
            out_specs=pl.BlockSpec((tm, D), lambda i, src, nv: (i, 0)),
            scratch_shapes=[pltpu.VMEM((2, tm, D), F32), pltpu.SemaphoreType.DMA((2,))]),
        out_shape=jax.ShapeDtypeStruct((P, D), BF16),
        compiler_params=_cparams(("arbitrary",)),
        name="moe_gather",
    )(row_src, nvalid, h2)


def _moe_ffn_body(te_ref, nv_ref, x_ref, w1_ref, w3_ref, w2_ref, o_ref, w1b, w3b, w2b):
    i = pl.program_id(0)
    used = i < nv_ref[0]
    new_expert = (i == 0) | (te_ref[i] != te_ref[jnp.maximum(i - 1, 0)])

    @pl.when(used & new_expert)
    def _():
        w1b[...] = w1_ref[0, 0].astype(BF16)
        w3b[...] = w3_ref[0, 0].astype(BF16)
        w2b[...] = w2_ref[0, 0].astype(BF16)

    @pl.when(used)
    def _():
        x = x_ref[...]
        a = _silu(_dot(x, w1b[...])) * _dot(x, w3b[...])
        o_ref[...] = _dot(a.astype(BF16), w2b[...])

    @pl.when(jnp.logical_not(used))
    def _():
        o_ref[...] = jnp.zeros_like(o_ref)


def _moe_ffn(xs, tile_expert, nvalid, w1, w3, w2, layer, tm):
    P, D = xs.shape
    F = w1.shape[3]
    wmap = lambda i, te, nv: (layer, te[i], 0, 0)
    once = pl.Buffered(1)
    return pl.pallas_call(
        _moe_ffn_body,
        grid_spec=pltpu.PrefetchScalarGridSpec(
            num_scalar_prefetch=2,
            grid=(P // tm,),
            in_specs=[pl.BlockSpec((tm, D), lambda i, te, nv: (i, 0)),
                      pl.BlockSpec((1, 1, D, F), wmap, pipeline_mode=once),
                      pl.BlockSpec((1, 1, D, F), wmap, pipeline_mode=once),
                      pl.BlockSpec((1, 1, F, D), wmap, pipeline_mode=once)],
            out_specs=pl.BlockSpec((tm, D), lambda i, te, nv: (i, 0)),
            scratch_shapes=[pltpu.VMEM((D, F), BF16), pltpu.VMEM((D, F), BF16), pltpu.VMEM((F, D), BF16)]),
        out_shape=jax.ShapeDtypeStruct((P, D), F32),
        compiler_params=_cparams(("arbitrary",)),
        name="moe_ffn",
    )(tile_expert, nvalid, xs, w1, w3, w2)


def _combine_body(alpha, tr, has_next, pos_ref, ys_hbm, x_ref, gt_ref, m_ref, g_ref, b_ref, *rest):
    if has_next:
        mn_ref, x2_ref, hn_ref, ya, yb, sem = rest
    else:
        x2_ref, ya, yb, sem = rest
    i = pl.program_id(0)
    slot = lax.rem(i, 2)

    def copies(tile, sl, j):
        r = 2 * (tile * tr + j)
        return (_row_copy(ys_hbm, pos_ref[r], ya.at[sl], j, sem.at[sl]),
                _row_copy(ys_hbm, pos_ref[r + 1], yb.at[sl], j, sem.at[sl]))

    def issue(tile, sl):
        def f(j, c):
            for cp in copies(tile, sl, j):
                cp.start()
            return c
        lax.fori_loop(0, tr, f, 0, unroll=DMA_UNROLL)

    @pl.when(i == 0)
    def _():
        issue(0, 0)

    @pl.when(i + 1 < pl.num_programs(0))
    def _():
        issue(i + 1, 1 - slot)

    def drain(j, c):
        for cp in copies(i, slot, j):
            cp.wait()
        return c
    lax.fori_loop(0, tr, drain, 0, unroll=DMA_UNROLL)
    gt = gt_ref[...]
    y = gt[:, 0:1] * ya[slot] + gt[:, 1:2] * yb[slot]
    m = m_ref[0]
    x2 = _layernorm(alpha * x_ref[...] + m[5:6, :] * y, g_ref[...], b_ref[...])
    x2_ref[...] = x2
    if has_next:
        mn = mn_ref[0]
        hn_ref[...] = (x2 * (1.0 + mn[1:2, :]) + mn[0:1, :]).astype(hn_ref.dtype)


def _moe_combine(ys, pos, gates, x1, mod3, ln_g, ln_b, mod3_next, alpha, T, B):
    has_next = mod3_next is not None
    D = x1.shape[1]
    NA = x1.shape[0] if has_next else B * T
    tr = _pick(NA, (256, 128))
    row = lambda i, p: (i, 0)
    fix = lambda i, p: (0, 0)
    seg = _seg_map(tr, T, B)
    in_specs = [pl.BlockSpec(memory_space=pl.ANY),
                pl.BlockSpec((tr, D), row), pl.BlockSpec((tr, LANES), row),
                pl.BlockSpec((1, 6, D), seg),
                pl.BlockSpec((1, D), fix), pl.BlockSpec((1, D), fix)]
    args = [pos, ys, x1, gates, mod3, ln_g.reshape(1, D), ln_b.reshape(1, D)]
    out_specs = [pl.BlockSpec((tr, D), row)]
    out_shape = [jax.ShapeDtypeStruct((NA, D), F32)]
    if has_next:
        in_specs.append(pl.BlockSpec((1, 6, D), seg))
        args.append(mod3_next)
        out_specs.append(pl.BlockSpec((tr, D), row))
        out_shape.append(jax.ShapeDtypeStruct((NA, D), BF16))
    return pl.pallas_call(
        functools.partial(_combine_body, alpha, tr, has_next),
        grid_spec=pltpu.PrefetchScalarGridSpec(
            num_scalar_prefetch=1,
            grid=(NA // tr,),
            in_specs=in_specs,
            out_specs=out_specs,
            scratch_shapes=[pltpu.VMEM((2, tr, D), F32), pltpu.VMEM((2, tr, D), F32),
                            pltpu.SemaphoreType.DMA((2,))]),
        out_shape=out_shape,
        compiler_params=_cparams(("arbitrary",)),
        name="moe_combine",
    )(*args)


def _route(logits):
    lg = logits[:, :MOE_GROUPS]
    le_all = logits[:, MOE_GROUPS:MOE_GROUPS + MOE_EXPERTS]
    n = lg.shape[0]
    grp = jnp.argmax(lg, -1)
    p_grp = jnp.take_along_axis(jax.nn.softmax(lg, -1), grp[:, None], axis=1)
    le = jnp.take_along_axis(le_all.reshape(n, MOE_GROUPS, MOE_EPG), grp[:, None, None], axis=1)[:, 0]
    top_v, top_i = lax.top_k(le, 2)
    w_sel = p_grp * jax.nn.softmax(top_v, -1)
    eid = grp[:, None] * MOE_EPG + top_i
    return eid.astype(jnp.int32), w_sel


def _dispatch(eid, tm):
    flat = eid.reshape(-1)
    na = flat.shape[0]
    experts = jnp.arange(MOE_EXPERTS, dtype=jnp.int32)
    order = jnp.argsort(flat, stable=True).astype(jnp.int32)
    inv = jnp.argsort(order).astype(jnp.int32)
    counts = jnp.sum((flat[:, None] == experts[None, :]).astype(jnp.int32), axis=0)
    starts = jnp.cumsum(counts) - counts
    ptiles = (counts + tm - 1) // tm
    pstart_t = jnp.cumsum(ptiles) - ptiles
    pos = (pstart_t[flat] * tm + (inv - starts[flat])).astype(jnp.int32)
    nt = na // tm + MOE_EXPERTS
    nvalid = jnp.sum(ptiles).astype(jnp.int32)
    tile_ids = jnp.arange(nt, dtype=jnp.int32)
    tile_expert = jnp.sum((tile_ids[:, None] >= (pstart_t + ptiles)[None, :]).astype(jnp.int32), axis=1)
    tile_expert = jnp.minimum(tile_expert, MOE_EXPERTS - 1)
    last_e = tile_expert[jnp.maximum(nvalid - 1, 0)]
    tile_expert = jnp.where(tile_ids < nvalid, tile_expert, last_e)
    prow = jnp.arange(nt * tm, dtype=jnp.int32)
    pe = jnp.repeat(tile_expert, tm)
    within = jnp.minimum(prow - pstart_t[pe] * tm, jnp.maximum(counts[pe] - 1, 0))
    row_src = order[jnp.clip(starts[pe] + within, 0, na - 1)] // 2
    return row_src.astype(jnp.int32), pos, tile_expert.astype(jnp.int32), nvalid.reshape(1)


def _rope_tables(T):
    rows = T // GRID_W
    row = jnp.repeat(jnp.arange(rows, dtype=F32), GRID_W)
    col = jnp.tile(jnp.arange(GRID_W, dtype=F32), rows)
    axis_dim = HEAD_DIM // 2
    inv = ROPE_THETA ** (-jnp.arange(0, axis_dim, 2, dtype=F32) / axis_dim)
    ang = jnp.concatenate([row[:, None] * inv, col[:, None] * inv], -1)
    c, s = jnp.cos(ang), jnp.sin(ang)
    return jnp.concatenate([c, c], -1), jnp.concatenate([-s, s], -1)


def kernel(x, c, ctx, c_ctx, w_mod, b_mod, w_in, gdn_conv, gdn_a_log, gdn_dt_bias, gdn_norm, fn_w,
           q_norm, k_norm, w_out, ln1_g, ln1_b, ln2_g, ln2_b, router_g, router_g_b, router_e,
           router_e_b, w1, w3, w2):
    B, T, D = x.shape
    Tc = ctx.shape[1]
    L = w_mod.shape[0]
    NL, NC = B * T, B * Tc
    alpha = (2 * L) ** 0.25
    H = GDN_HEADS

    xa = jnp.concatenate([x.reshape(NL, D), ctx.reshape(NC, D)], axis=0)
    sc = jax.nn.silu(jnp.concatenate([c, c_ctx[None, :]], axis=0))
    sc8 = jnp.pad(sc, ((0, SUBLANES - B - 1), (0, 0)))
    mod = _modulation(sc8, w_mod, b_mod)
    cs_tab, sn_tab = _rope_tables(T)
    w1st, m2, m2c, cs_dft = _fn_tables(T, Tc)
    T1, T1c = T // LANES, Tc // LANES
    tm_moe = 256

    h = None
    for l in range(L):
        last = l == L - 1
        mod3 = mod[l].reshape(SUBLANES, 6, D)
        if h is None:
            h = _modcast(xa, mod3, T, B)
        wl = w_in[l]
        o = np.cumsum([0, 3 * GDN_WIDTH, GDN_WIDTH, 2 * H, 2 * H, FN_WIDTH, GQA_WIDTH, GQA_KV_WIDTH,
                       GQA_KV_WIDTH])
        w_main = jnp.concatenate([wl[:, o[0]:o[2]], wl[:, o[4]:o[8]]], axis=1).astype(BF16)
        wa, wb = wl[:, o[2]:o[3]], wl[:, o[3]:o[4]]
        zpad = jnp.zeros((D, LANES - 2 * H), F32)
        w_ab = jnp.concatenate([wa[:, :H], wb[:, :H], zpad, wa[:, H:], wb[:, H:], zpad], axis=1).astype(BF16)
        p_main = _mm([h], [w_main], F32, "in_proj")
        p_ab = _mm([h], [w_ab], F32, "in_proj_gates")
        qkvn = _gdn_inputs(p_main, gdn_conv[l], B, T, Tc)
        prow = jnp.pad(jnp.stack([gdn_a_log[l], gdn_dt_bias[l]], axis=1),
                       ((0, 0), (0, SUBLANES - 2), (0, LANES - H)))
        pcol = jnp.swapaxes(prow, 1, 2)
        p_abt = jnp.swapaxes(p_ab.reshape(-1, GDN_CHUNK, 2 * LANES), 1, 2)
        o_fwd, o_bwd = _gdn_scan(*_gdn_chunks(qkvn, p_ab, p_abt, prow, pcol), B, T, Tc)
        gdn_y = _gdn_output(o_fwd, o_bwd, p_main, gdn_norm[l])
        f = p_main[:, OFF_F:OFF_F + FN_WIDTH]
        fnw = fn_w[l].astype(BF16)
        z = _fn_stage1(f[:NL].reshape(B, T1, LANES * FN_WIDTH), w1st)
        fn_l = _fn_stage2(z.reshape(B, 2, T1, LANES, FN_WIDTH), m2, cs_dft, fnw, same_z=False)
        fn_c = _fn_stage2(f[NL:].reshape(B, 2, 1, LANES, FN_WIDTH), m2c, cs_dft, fnw, same_z=True)
        fn_y = jnp.concatenate([fn_l.reshape(NL, FN_WIDTH), fn_c.reshape(NC, FN_WIDTH)], axis=0)
        qk_att, vt_att = _att_prep(p_main, cs_tab, sn_tab, q_norm[l], k_norm[l], B, T)
        at_y = jnp.concatenate([_flash(qk_att, vt_att, B, T, Tc, True),
                                _flash(qk_att, vt_att, B, T, Tc, False)], axis=0)
        wo = w_out[l]
        ws = [wo[:GDN_WIDTH].astype(BF16), wo[GDN_WIDTH:GDN_WIDTH + FN_WIDTH].astype(BF16),
              wo[GDN_WIDTH + FN_WIDTH:].astype(BF16)]
        y = _mm([gdn_y, fn_y, at_y], ws, F32, "out_proj")
        rw = jnp.concatenate([router_g[l], router_e[l],
                              jnp.zeros((D, LANES - MOE_GROUPS - MOE_EXPERTS), F32)], axis=1)
        rb = jnp.concatenate([router_g_b[l], router_e_b[l],
                              jnp.zeros((LANES - MOE_GROUPS - MOE_EXPERTS,), F32)])[None, :]
        x1, h2, logits = _ln1_router(y, xa, mod3, ln1_g[l], ln1_b[l], rw, rb, alpha, T, B)
        eid, w_sel = _route(logits)
        row_src, pos, tile_expert, nvalid = _dispatch(eid, tm_moe)
        xs = _moe_gather(h2, row_src, nvalid, tm_moe)
        ys = _moe_ffn(xs, tile_expert, nvalid, w1, w3, w2, l, tm_moe)
        gates = jnp.pad(w_sel, ((0, 0), (0, LANES - 2)))
        mod3_next = None if last else mod[l + 1].reshape(SUBLANES, 6, D)
        res = _moe_combine(ys, pos, gates, x1, mod3, ln2_g[l], ln2_b[l], mod3_next, alpha, T, B)
        xa = res[0]
        h = None if last else res[1]
    return xa.reshape(B, T, D)
```

```python
import functools
import math

import numpy as np
import jax
import jax.numpy as jnp
from jax import lax
from jax.experimental import pallas as pl
from jax.experimental.pallas import tpu as pltpu

F32 = jnp.float32
BF16 = jnp.bfloat16

HEAD_DIM = 128
GDN_HEADS = 12
GDN_WIDTH = GDN_HEADS * HEAD_DIM
GDN_CONV_W = 5
GDN_CHUNK = 64
GDN_INV_BLOCK = 8
FN_GROUPS = 8
FN_WIDTH = FN_GROUPS * HEAD_DIM
GQA_HEADS = 12
GQA_KV_HEADS = 4
GQA_GROUP = GQA_HEADS // GQA_KV_HEADS
GQA_WIDTH = GQA_HEADS * HEAD_DIM
GQA_KV_WIDTH = GQA_KV_HEADS * HEAD_DIM
GRID_W = 64
ROPE_THETA = 10000.0
MOE_GROUPS = 4
MOE_EPG = 8
MOE_EXPERTS = MOE_GROUPS * MOE_EPG
LN_EPS = 1e-5
RMS_EPS = 1e-6
LANES = 128
SUBLANES = 8
VMEM_LIMIT = 56 * 1024 * 1024
DMA_UNROLL = 8
ATT_ROW_BLOCK = 64
ATT_VT_ROWS = HEAD_DIM + 16

OFF_QKV = 0
OFF_Z = 3 * GDN_WIDTH
OFF_F = OFF_Z + GDN_WIDTH
OFF_Q = OFF_F + FN_WIDTH
OFF_K = OFF_Q + GQA_WIDTH
OFF_V = OFF_K + GQA_KV_WIDTH
N_MAIN = OFF_V + GQA_KV_WIDTH


def _cparams(sem, **kw):
    return pltpu.CompilerParams(dimension_semantics=sem, vmem_limit_bytes=VMEM_LIMIT, **kw)


def _pick(n, cands):
    for c in cands:
        if n % c == 0:
            return c
    raise ValueError(f"no tile for {n} in {cands}")


def _dot(a, b):
    return jnp.dot(a, b, preferred_element_type=F32)


def _dot_nt(a, b):
    return lax.dot_general(a, b, (((1,), (1,)), ((), ())), preferred_element_type=F32)


def _split2(a):
    hi = a.astype(BF16)
    lo = (a - hi.astype(F32)).astype(BF16)
    return hi, lo


def _dot3(a, b):
    ah, al = _split2(a)
    bh, bl = _split2(b)
    return _dot(ah, bh) + (_dot(ah, bl) + _dot(al, bh))


def _silu(x):
    return x / (1.0 + jnp.exp(-x))


def _mod_body(x_ref, w_ref, b_ref, o_ref):
    o_ref[0] = _dot3(x_ref[...], w_ref[0]) + b_ref[0]


def _modulation(sc8, w_mod, b_mod):
    L, D, N = w_mod.shape
    tn = _pick(N, (512, 256, 128))
    return pl.pallas_call(
        _mod_body,
        grid=(L, N // tn),
        in_specs=[pl.BlockSpec((SUBLANES, D), lambda l, j: (0, 0)),
                  pl.BlockSpec((1, D, tn), lambda l, j: (l, 0, j)),
                  pl.BlockSpec((1, 1, tn), lambda l, j: (l, 0, j))],
        out_specs=pl.BlockSpec((1, SUBLANES, tn), lambda l, j: (l, 0, j)),
        out_shape=jax.ShapeDtypeStruct((L, SUBLANES, N), F32),
        compiler_params=_cparams(("arbitrary", "arbitrary")),
        name="modulation",
    )(sc8, w_mod, b_mod.reshape(L, 1, N))


def _seg_map(tr, T, B):
    return lambda i, *_: (jnp.minimum((i * tr) // T, B), 0, 0)


def _modcast_body(x_ref, m_ref, o_ref):
    m = m_ref[0]
    o_ref[...] = (x_ref[...] * (1.0 + m[1:2, :]) + m[0:1, :]).astype(o_ref.dtype)


def _modcast(xa, mod3, T, B):
    NA, D = xa.shape
    tr = _pick(NA, (256, 128))
    return pl.pallas_call(
        _modcast_body,
        grid=(NA // tr,),
        in_specs=[pl.BlockSpec((tr, D), lambda i: (i, 0)),
                  pl.BlockSpec((1, 6, D), _seg_map(tr, T, B))],
        out_specs=pl.BlockSpec((tr, D), lambda i: (i, 0)),
        out_shape=jax.ShapeDtypeStruct((NA, D), BF16),
        compiler_params=_cparams(("arbitrary",)),
        name="modcast",
    )(xa, mod3)


def _mm_body(n, out_dtype, *refs):
    xs, ws, o_ref = refs[:n], refs[n:2 * n], refs[2 * n]
    acc = _dot(xs[0][...], ws[0][...])
    for x_ref, w_ref in zip(xs[1:], ws[1:]):
        acc = acc + _dot(x_ref[...], w_ref[...])
    o_ref[...] = acc.astype(out_dtype)


def _mm(xs, ws, out_dtype, name):
    M = xs[0].shape[0]
    N = ws[0].shape[1]
    tm = _pick(M, (1536, 1024, 768, 512, 256))
    tn = _pick(N, (512, 256, 128))
    n = len(xs)
    in_specs = ([pl.BlockSpec((tm, x.shape[1]), lambda i, j: (i, 0)) for x in xs]
                + [pl.BlockSpec((w.shape[0], tn), lambda i, j: (0, j)) for w in ws])
    return pl.pallas_call(
        functools.partial(_mm_body, n, out_dtype),
        grid=(M // tm, N // tn),
        in_specs=in_specs,
        out_specs=pl.BlockSpec((tm, tn), lambda i, j: (i, j)),
        out_shape=jax.ShapeDtypeStruct((M, N), out_dtype),
        compiler_params=_cparams(("arbitrary", "arbitrary")),
        name=name,
    )(*xs, *ws)


def _gdn_in_body(B, T, Tc, tr, prev_ref, cur_ref, next_ref, w_ref, o_ref, buf):
    i = pl.program_id(0)
    j = pl.program_id(1)
    row0 = i * tr
    nl = B * T
    is_lat = row0 < nl
    r = jnp.where(is_lat, lax.rem(row0, T), lax.rem(row0 - nl, Tc))
    seq = jnp.where(is_lat, T, Tc)
    at_start = r == 0
    at_end = r + tr == seq
    buf[0:SUBLANES, :] = jnp.where(at_start, 0.0, prev_ref[...])
    buf[SUBLANES:SUBLANES + tr, :] = cur_ref[...]
    buf[SUBLANES + tr:2 * SUBLANES + tr, :] = jnp.where(at_end, 0.0, next_ref[...])
    pad = GDN_CONV_W // 2
    acc = buf[pl.ds(SUBLANES - pad, tr), :] * w_ref[0:1, :]
    for t in range(1, GDN_CONV_W):
        acc = acc + buf[pl.ds(SUBLANES - pad + t, tr), :] * w_ref[t:t + 1, :]
    act = _silu(acc)
    scale = jnp.where(j == 0, HEAD_DIM ** -0.5, 1.0)
    do_norm = j < 2
    for h in range(GDN_HEADS):
        a = act[:, h * HEAD_DIM:(h + 1) * HEAD_DIM]
        nrm = a * lax.rsqrt(jnp.sum(a * a, axis=-1, keepdims=True) + RMS_EPS) * scale
        o_ref[:, h * HEAD_DIM:(h + 1) * HEAD_DIM] = jnp.where(do_norm, nrm, a)


def _gdn_inputs(p_main, conv_w, B, T, Tc):
    NA = p_main.shape[0]
    tr = 256
    assert T % tr == 0 and Tc % tr == 0
    nb8 = NA // SUBLANES
    r8 = tr // SUBLANES
    W = GDN_WIDTH
    return pl.pallas_call(
        functools.partial(_gdn_in_body, B, T, Tc, tr),
        grid=(NA // tr, 3),
        in_specs=[pl.BlockSpec((SUBLANES, W), lambda i, j: (jnp.maximum(i * r8 - 1, 0), j)),
                  pl.BlockSpec((tr, W), lambda i, j: (i, j)),
                  pl.BlockSpec((SUBLANES, W), lambda i, j: (jnp.minimum((i + 1) * r8, nb8 - 1), j)),
                  pl.BlockSpec((GDN_CONV_W, W), lambda i, j: (0, j))],
        out_specs=pl.BlockSpec((tr, W), lambda i, j: (i, j)),
        out_shape=jax.ShapeDtypeStruct((NA, 3 * W), F32),
        scratch_shapes=[pltpu.VMEM((tr + 2 * SUBLANES, W), F32)],
        compiler_params=_cparams(("arbitrary", "arbitrary")),
        name="gdn_inputs",
    )(p_main, p_main, p_main, conv_w)


def _softplus(x):
    return jnp.maximum(x, 0.0) + jnp.log1p(jnp.exp(-jnp.abs(x)))


def _sigmoid(x):
    return 1.0 / (1.0 + jnp.exp(-x))


def _split3(a):
    hi = a.astype(BF16)
    r1 = a - hi.astype(F32)
    mid = r1.astype(BF16)
    lo = (r1 - mid.astype(F32)).astype(BF16)
    return hi, mid, lo


def _gdn_chunk_body(q_ref, k_ref, v_ref, ab_ref, abt_ref, prow_ref, pcol_ref,
                    u_ref, wq_ref, qkd_ref, kdt_ref, eg_ref):
    C = GDN_CHUNK
    H = GDN_HEADS
    heads = range(H)
    ii = lax.broadcasted_iota(jnp.int32, (C, C), 0)
    jj = lax.broadcasted_iota(jnp.int32, (C, C), 1)
    eye = jnp.where(ii == jj, 1.0, 0.0)
    bs = GDN_INV_BLOCK
    diag_blk = (ii // bs) == (jj // bs)
    pair_off = []
    while bs < C:
        pair_off.append(((ii // bs) ^ (jj // bs)) == 1)
        bs *= 2

    sl = [slice(h * HEAD_DIM, (h + 1) * HEAD_DIM) for h in heads]
    kb = [k_ref[:, sl[h]].astype(BF16) for h in heads]
    kk = [_dot_nt(kb[h], kb[h]) for h in heads]
    qk = [_dot_nt(q_ref[:, sl[h]].astype(BF16), kb[h]) for h in heads]

    dh = [(d, h) for d in range(2) for h in heads]
    incl, strict, beta, gc, gct, gtot = {}, {}, {}, {}, {}, {}
    for d in range(2):
        ahead = (ii - jj) if d == 0 else (jj - ii)
        incl[d] = ahead >= 0
        strict[d] = ahead > 0
        tri = jnp.where(incl[d], 1.0, 0.0).astype(BF16)
        tri_t = jnp.where(ahead <= 0, 1.0, 0.0).astype(BF16)
        ab = ab_ref[:, d * LANES:(d + 1) * LANES]
        abt = abt_ref[0, d * LANES:(d + 1) * LANES, :]
        prow = prow_ref[d]
        pcol = pcol_ref[d]
        g = -jnp.exp(prow[0:1, :]) * _softplus(ab + prow[1:2, :])
        gt = -jnp.exp(pcol[:, 0:1]) * _softplus(abt + pcol[:, 1:2])
        beta[d] = _sigmoid(ab)
        g1, g2, g3 = _split3(g)
        gc[d] = _dot(tri, g1) + (_dot(tri, g2) + _dot(tri, g3))
        t1, t2, t3 = _split3(gt)
        gct[d] = _dot(t1, tri_t) + (_dot(t2, tri_t) + _dot(t3, tri_t))
        gtot[d] = jnp.sum(g, axis=0, keepdims=True)
        eg_ref[d, 0] = jnp.exp(gtot[d])

    bcol = {x: jnp.broadcast_to(beta[x[0]][:, H + x[1]:H + x[1] + 1], (C, HEAD_DIM)) for x in dh}
    gcol = {x: jnp.broadcast_to(gc[x[0]][:, x[1]:x[1] + 1], (C, HEAD_DIM)) for x in dh}
    decay = {(d, h): jnp.where(incl[d], jnp.exp(jnp.minimum(gcol[(d, h)][:, :C] - gct[d][h:h + 1, :], 0.0)), 0.0)
             for d, h in dh}
    a_mat = {(d, h): jnp.where(strict[d], kk[h] * decay[(d, h)] * bcol[(d, h)][:, :C], 0.0) for d, h in dh}
    for d, h in dh:
        qkd_ref[d, :, h * C:(h + 1) * C] = (qk[h] * decay[(d, h)]).astype(qkd_ref.dtype)
    p = {x: jnp.where(diag_blk, -a_mat[x], 0.0) for x in dh}
    tm = {x: eye + p[x] for x in dh}
    for _ in range(int(math.log2(GDN_INV_BLOCK)) - 1):
        pb = {x: p[x].astype(BF16) for x in dh}
        p = {x: _dot(pb[x], pb[x]) for x in dh}
        tm = {x: tm[x] + _dot(tm[x].astype(BF16), p[x].astype(BF16)) for x in dh}
    for off in pair_off:
        tb = {x: tm[x].astype(BF16) for x in dh}
        at = {x: _dot(jnp.where(off, a_mat[x], 0.0).astype(BF16), tb[x]) for x in dh}
        tm = {x: tm[x] - _dot(tb[x], at[x].astype(BF16)) for x in dh}
    eg = {x: jnp.exp(gcol[x]) for x in dh}
    k = [k_ref[:, sl[h]] for h in heads]
    rhs = {(d, h): jnp.concatenate([v_ref[:, sl[h]] * bcol[(d, h)], k[h] * (bcol[(d, h)] * eg[(d, h)])],
                                   axis=1).astype(BF16) for d, h in dh}
    uw = {x: _dot(tm[x].astype(BF16), rhs[x]) for x in dh}
    for d, h in dh:
        x = (d, h)
        u_ref[d, :, sl[h]] = uw[x][:, :HEAD_DIM]
        wq_ref[d, 0, h, :C, :] = uw[x][:, HEAD_DIM:].astype(wq_ref.dtype)
        wq_ref[d, 0, h, C:, :] = (q_ref[:, sl[h]] * eg[x]).astype(wq_ref.dtype)
        kdec = k[h] * jnp.exp(gtot[d][:, h:h + 1] - gcol[x])
        kdt_ref[d, 0, h] = kdec.T.astype(kdt_ref.dtype)


def _gdn_chunks(qkvn, p_ab, p_abt, prow, pcol):
    NA = qkvn.shape[0]
    C = GDN_CHUNK
    H = GDN_HEADS
    W = GDN_WIDTH
    n = NA // C
    return pl.pallas_call(
        _gdn_chunk_body,
        grid=(n,),
        in_specs=[pl.BlockSpec((C, W), lambda i: (i, 0)),
                  pl.BlockSpec((C, W), lambda i: (i, 1)),
                  pl.BlockSpec((C, W), lambda i: (i, 2)),
                  pl.BlockSpec((C, 2 * LANES), lambda i: (i, 0)),
                  pl.BlockSpec((1, 2 * LANES, C), lambda i: (i, 0, 0)),
                  pl.BlockSpec((2, SUBLANES, LANES), lambda i: (0, 0, 0)),
                  pl.BlockSpec((2, LANES, SUBLANES), lambda i: (0, 0, 0))],
        out_specs=[pl.BlockSpec((2, C, W), lambda i: (0, i, 0)),
                   pl.BlockSpec((2, 1, H, 2 * C, HEAD_DIM), lambda i: (0, i, 0, 0, 0)),
                   pl.BlockSpec((2, C, H * C), lambda i: (0, i, 0)),
                   pl.BlockSpec((2, 1, H, HEAD_DIM, C), lambda i: (0, i, 0, 0, 0)),
                   pl.BlockSpec((2, 1, 1, LANES), lambda i: (0, i, 0, 0))],
        out_shape=[jax.ShapeDtypeStruct((2, NA, W), F32),
                   jax.ShapeDtypeStruct((2, n, H, 2 * C, HEAD_DIM), BF16),
                   jax.ShapeDtypeStruct((2, NA, H * C), BF16),
                   jax.ShapeDtypeStruct((2, n, H, HEAD_DIM, C), BF16),
                   jax.ShapeDtypeStruct((2, n, 1, LANES), F32)],
        compiler_params=_cparams(("arbitrary",)),
        name="gdn_chunks",
    )(qkvn, qkvn, qkvn, p_ab, p_abt, prow, pcol)


def _gdn_scan_body(*refs):
    C = GDN_CHUNK
    H = GDN_HEADS
    ins, (o0_ref, o1_ref, s_ref) = refs[:10], refs[10:]
    o_refs = (o0_ref, o1_ref)

    @pl.when(pl.program_id(1) == 0)
    def _():
        s_ref[...] = jnp.zeros_like(s_ref)

    dh = [(d, h) for d in range(2) for h in range(H)]
    u_ref, wq_ref, qkd_ref, kdt_ref, eg_ref = [ins[2 * j:2 * j + 2] for j in range(5)]
    st = {x: s_ref[x[0], x[1]] for x in dh}
    wqs = {(d, h): _dot(wq_ref[d][0, 0, h], st[(d, h)].astype(BF16)) for d, h in dh}
    vnb = {}
    for d, h in dh:
        v_new = u_ref[d][0, :, h * HEAD_DIM:(h + 1) * HEAD_DIM] - wqs[(d, h)][:C]
        vnb[(d, h)] = v_new.astype(BF16)
    for d, h in dh:
        o = wqs[(d, h)][C:] + _dot(qkd_ref[d][0, :, h * C:(h + 1) * C], vnb[(d, h)])
        o_refs[d][:, h * HEAD_DIM:(h + 1) * HEAD_DIM] = o
    for d, h in dh:
        e = eg_ref[d][0, 0][:, h:h + 1]
        s_ref[d, h] = st[(d, h)] * e + _dot(kdt_ref[d][0, 0, h], vnb[(d, h)])


def _gdn_scan(u, wq, qkd, kdt, eg, B, T, Tc):
    NA = u.shape[1]
    C = GDN_CHUNK
    H = GDN_HEADS
    nl, nc = T // C, Tc // C
    W = GDN_WIDTH

    def rb(d):
        def f(b, s):
            ctx_blk = B * nl + b * nc + (s if d == 0 else nc - 1 - s)
            lat_blk = b * nl + ((s - nc) if d == 0 else nl - 1 - (s - nc))
            return jnp.where(s < nc, ctx_blk, lat_blk)
        return f

    in_specs, args = [], []
    for arr, blk, imap in (
            (u, (1, C, W), lambda d: (lambda b, s: (d, rb(d)(b, s), 0))),
            (wq, (1, 1, H, 2 * C, HEAD_DIM), lambda d: (lambda b, s: (d, rb(d)(b, s), 0, 0, 0))),
            (qkd, (1, C, H * C), lambda d: (lambda b, s: (d, rb(d)(b, s), 0))),
            (kdt, (1, 1, H, HEAD_DIM, C), lambda d: (lambda b, s: (d, rb(d)(b, s), 0, 0, 0))),
            (eg, (1, 1, 1, LANES), lambda d: (lambda b, s: (d, rb(d)(b, s), 0, 0)))):
        for d in range(2):
            in_specs.append(pl.BlockSpec(blk, imap(d)))
            args.append(arr)
    return pl.pallas_call(
        _gdn_scan_body,
        grid=(B, nc + nl),
        in_specs=in_specs,
        out_specs=[pl.BlockSpec((C, W), lambda b, s: (rb(0)(b, s), 0)),
                   pl.BlockSpec((C, W), lambda b, s: (rb(1)(b, s), 0))],
        out_shape=[jax.ShapeDtypeStruct((NA, W), F32), jax.ShapeDtypeStruct((NA, W), F32)],
        scratch_shapes=[pltpu.VMEM((2, H, HEAD_DIM, HEAD_DIM), F32)],
        compiler_params=_cparams(("arbitrary", "arbitrary")),
        name="gdn_scan",
    )(*args)


def _gdn_out_body(o0_ref, o1_ref, z_ref, w_ref, y_ref):
    o = o0_ref[...] + o1_ref[...]
    z = z_ref[...]
    w = w_ref[...]
    for h in range(GDN_HEADS):
        lo, hi = h * HEAD_DIM, (h + 1) * HEAD_DIM
        a = o[:, lo:hi]
        n = a * lax.rsqrt(jnp.mean(a * a, axis=-1, keepdims=True) + RMS_EPS) * w
        y_ref[:, lo:hi] = (n * _silu(z[:, lo:hi])).astype(y_ref.dtype)


def _gdn_output(o_fwd, o_bwd, p_main, norm_w):
    NA = p_main.shape[0]
    tr = 256
    W = GDN_WIDTH
    return pl.pallas_call(
        _gdn_out_body,
        grid=(NA // tr,),
        in_specs=[pl.BlockSpec((tr, W), lambda i: (i, 0)),
                  pl.BlockSpec((tr, W), lambda i: (i, 0)),
                  pl.BlockSpec((tr, W), lambda i: (i, OFF_Z // W)),
                  pl.BlockSpec((1, HEAD_DIM), lambda i: (0, 0))],
        out_specs=pl.BlockSpec((tr, W), lambda i: (i, 0)),
        out_shape=jax.ShapeDtypeStruct((NA, W), BF16),
        compiler_params=_cparams(("arbitrary",)),
        name="gdn_output",
    )(o_fwd, o_bwd, p_main, norm_w.reshape(1, HEAD_DIM))


def _fn_stage1_body(w_ref, x_ref, z_ref):
    t1 = x_ref.shape[1]
    z = _dot3(w_ref[...], x_ref[0])
    z_ref[0] = z.reshape(2, t1, z.shape[-1])


def _fn_stage1(xf, w1st):
    B, T1, NN = xf.shape
    tn = _pick(NN, (8192, 4096, 1024))
    return pl.pallas_call(
        _fn_stage1_body,
        grid=(B, NN // tn),
        in_specs=[pl.BlockSpec((2 * T1, T1), lambda b, j: (0, 0)),
                  pl.BlockSpec((1, T1, tn), lambda b, j: (b, 0, j))],
        out_specs=pl.BlockSpec((1, 2, T1, tn), lambda b, j: (b, 0, 0, j)),
        out_shape=jax.ShapeDtypeStruct((B, 2, T1, NN), F32),
        compiler_params=_cparams(("arbitrary", "arbitrary")),
        name="fnet_stage1",
    )(w1st, xf)


def _fn_stage2_body(m_ref, z_ref, cs_ref, fw_ref, o_ref):
    z = z_ref[0, :, 0].reshape(2 * LANES, FN_WIDTH)
    hh = _dot3(m_ref[0], z)
    hr, hi = hh[:LANES], hh[LANES:]
    cs = cs_ref[...]
    cols = []
    for g in range(FN_GROUPS):
        lo, up = g * HEAD_DIM, (g + 1) * HEAD_DIM
        cols.append(_dot3(jnp.concatenate([hr[:, lo:up], hi[:, lo:up]], axis=1), cs))
    fr = jnp.concatenate(cols, axis=1)
    o_ref[0] = _dot(fr.astype(BF16), fw_ref[...]).astype(o_ref.dtype)


def _fn_stage2(z5, m2, cs, fnw, same_z):
    B = z5.shape[0]
    T1 = m2.shape[0]
    zmap = (lambda b, k: (b, 0, 0, 0, 0)) if same_z else (lambda b, k: (b, 0, k, 0, 0))
    return pl.pallas_call(
        _fn_stage2_body,
        grid=(B, T1),
        in_specs=[pl.BlockSpec((1, 2 * LANES, 2 * LANES), lambda b, k: (k, 0, 0)),
                  pl.BlockSpec((1, 2, 1, LANES, FN_WIDTH), zmap),
                  pl.BlockSpec((2 * HEAD_DIM, HEAD_DIM), lambda b, k: (0, 0)),
                  pl.BlockSpec((FN_WIDTH, FN_WIDTH), lambda b, k: (0, 0))],
        out_specs=pl.BlockSpec((1, LANES, FN_WIDTH), lambda b, k: (b, 0, k)),
        out_shape=jax.ShapeDtypeStruct((B, LANES, T1 * FN_WIDTH), BF16),
        compiler_params=_cparams(("arbitrary", "arbitrary")),
        name="fnet_stage2",
    )(m2, z5, cs, fnw)


def _phase(num, den):
    ang = (2.0 * math.pi / den) * lax.rem(num, den).astype(F32)
    return jnp.cos(ang), jnp.sin(ang)


def _fn_tables(T, Tc):
    t1 = T // LANES
    a = jnp.arange(t1, dtype=jnp.int32)
    c1, s1 = _phase(a[:, None] * a[None, :], t1)
    w1st = jnp.concatenate([c1, -s1], axis=0)
    k1 = jnp.arange(t1, dtype=jnp.int32)[:, None, None]
    k2 = jnp.arange(LANES, dtype=jnp.int32)[None, :, None]
    t2 = jnp.arange(LANES, dtype=jnp.int32)[None, None, :]
    c, s = _phase(k2 * t2 * t1 + k1 * t2, T)
    sc = (T * HEAD_DIM) ** -0.5
    m2 = jnp.concatenate([jnp.concatenate([c, s], axis=2), jnp.concatenate([-s, c], axis=2)], axis=1) * sc
    t1c = Tc // LANES
    k = (jnp.arange(t1c, dtype=jnp.int32)[:, None, None]
         + t1c * jnp.arange(LANES, dtype=jnp.int32)[None, :, None])
    t = jnp.arange(Tc, dtype=jnp.int32)[None, None, :]
    cc, sc_ = _phase(k * t, Tc)
    m2c = jnp.concatenate([cc, -sc_], axis=1) * (Tc * HEAD_DIM) ** -0.5
    ch = jnp.arange(HEAD_DIM, dtype=jnp.int32)
    c3, s3 = _phase(ch[:, None] * ch[None, :], HEAD_DIM)
    cs = jnp.concatenate([c3, s3], axis=0)
    return w1st, m2, m2c, cs


def _att_prep_body(B, T, tr, x_ref, cs_ref, sn_ref, qw_ref, kw_ref, o_ref, vt_ref):
    i = pl.program_id(0)
    j = pl.program_id(1)
    nh = x_ref.shape[1] // HEAD_DIM

    @pl.when(j == 4)
    def _():
        vt = x_ref[...].T.astype(vt_ref.dtype)
        for h in range(nh):
            vt_ref[h * ATT_VT_ROWS:h * ATT_VT_ROWS + HEAD_DIM, :] = vt[h * HEAD_DIM:(h + 1) * HEAD_DIM]
            vt_ref[h * ATT_VT_ROWS + HEAD_DIM:(h + 1) * ATT_VT_ROWS, :] = jnp.ones(
                (ATT_VT_ROWS - HEAD_DIM, vt.shape[1]), vt_ref.dtype)

    @pl.when(j < 4)
    def _():
        is_lat = i * tr < B * T
        is_q = j < 3
        w = jnp.where(is_q, qw_ref[...], kw_ref[...])
        scale = jnp.where(is_q, HEAD_DIM ** -0.5 * math.log2(math.e), 1.0)
        cs = cs_ref[...]
        sn = sn_ref[...]
        for h in range(nh):
            lo, hi = h * HEAD_DIM, (h + 1) * HEAD_DIM
            a = x_ref[:, lo:hi]
            n = a * lax.rsqrt(jnp.mean(a * a, axis=-1, keepdims=True) + RMS_EPS) * w
            rot = n * cs + pltpu.roll(n, HEAD_DIM // 2, 1) * sn
            o_ref[:, lo:hi] = (jnp.where(is_lat, rot, n) * scale).astype(o_ref.dtype)


def _att_prep(p_main, cs_tab, sn_tab, qw, kw, B, T):
    NA = p_main.shape[0]
    tr = 256
    cw = GQA_KV_WIDTH
    base = OFF_Q // cw
    nrt = T // tr
    return pl.pallas_call(
        functools.partial(_att_prep_body, B, T, tr),
        grid=(NA // tr, 5),
        in_specs=[pl.BlockSpec((tr, cw), lambda i, j: (i, base + j)),
                  pl.BlockSpec((tr, HEAD_DIM), lambda i, j: (lax.rem(i, nrt), 0)),
                  pl.BlockSpec((tr, HEAD_DIM), lambda i, j: (lax.rem(i, nrt), 0)),
                  pl.BlockSpec((1, HEAD_DIM), lambda i, j: (0, 0)),
                  pl.BlockSpec((1, HEAD_DIM), lambda i, j: (0, 0))],
        out_specs=[pl.BlockSpec((tr, cw), lambda i, j: (i, jnp.minimum(j, 3))),
                   pl.BlockSpec((GQA_KV_HEADS * ATT_VT_ROWS, tr), lambda i, j: (0, i))],
        out_shape=[jax.ShapeDtypeStruct((NA, 4 * cw), BF16),
                   jax.ShapeDtypeStruct((GQA_KV_HEADS * ATT_VT_ROWS, NA), BF16)],
        compiler_params=_cparams(("arbitrary", "arbitrary")),
        name="attention_prep",
    )(p_main, cs_tab, sn_tab, qw.reshape(1, HEAD_DIM), kw.reshape(1, HEAD_DIM))


def _flash_body(tk, n_lat, *refs):
    if n_lat:
        q_ref, kc_ref, vtc_ref, kl_ref, vtl_ref, o_ref, qt_sc, s_sc, p_sc, m_sc, acc_sc = refs
    else:
        q_ref, kc_ref, vtc_ref, o_ref, qt_sc, s_sc, p_sc, m_sc, acc_sc = refs
    heads = range(GQA_GROUP)
    tq = q_ref.shape[0]
    for g in heads:
        qt_sc[g] = q_ref[:, g * HEAD_DIM:(g + 1) * HEAD_DIM].astype(F32).T.astype(BF16)
    m_sc[...] = jnp.full_like(m_sc, -jnp.inf)
    acc_sc[...] = jnp.zeros_like(acc_sc)

    def scores(slot, kc):
        for g in heads:
            s_sc[slot, g, :kc.shape[0], :] = _dot(kc, qt_sc[g])

    def softmax_pv(slot, vtc):
        nk = vtc.shape[1]
        blocks = [slice(r, r + ATT_ROW_BLOCK) for r in range(0, nk, ATT_ROW_BLOCK)]
        alphas = []
        for g in heads:
            cols = slice(g * tq, (g + 1) * tq)
            mx = s_sc[slot, g, blocks[0], :]
            for rb in blocks[1:]:
                mx = jnp.maximum(mx, s_sc[slot, g, rb, :])
            m_old = m_sc[g]
            m_new = jnp.maximum(m_old, jnp.max(mx, axis=0, keepdims=True))
            alphas.append(jnp.exp2(m_old - m_new))
            for rb in blocks:
                p_sc[slot, rb, cols] = jnp.exp2((s_sc[slot, g, rb, :] - m_new).astype(BF16))
            m_sc[g] = m_new
        acc_sc[...] = jnp.concatenate(alphas, axis=1) * acc_sc[...] + _dot(vtc, p_sc[slot, :nk, :])

    scores(0, kc_ref[...])
    softmax_pv(0, vtc_ref[...])
    if n_lat:
        def pair(c, carry):
            offs = [pl.multiple_of((2 * c + j) * tk, tk) for j in range(2)]
            for j in range(2):
                scores(j, kl_ref[pl.ds(offs[j], tk), :])
            for j in range(2):
                softmax_pv(j, vtl_ref[:, pl.ds(offs[j], tk)])
            return carry
        lax.fori_loop(0, n_lat // 2, pair, 0)
        if n_lat % 2:
            last = (n_lat - 1) * tk
            scores(0, kl_ref[last:last + tk, :])
            softmax_pv(0, vtl_ref[:, last:last + tk])
    for g in heads:
        cols = slice(g * tq, (g + 1) * tq)
        o = acc_sc[:HEAD_DIM, cols] / acc_sc[HEAD_DIM:HEAD_DIM + 1, cols]
        o_ref[:, g * HEAD_DIM:(g + 1) * HEAD_DIM] = o.T.astype(o_ref.dtype)


def _flash(qk_att, vt_att, B, T, Tc, latent):
    NL = B * T
    qb = GQA_GROUP * HEAD_DIM
    kcol = GQA_WIDTH // HEAD_DIM
    cb = NL // Tc
    if latent:
        tq = _pick(T, (512, 256, 128))
        tk = _pick(T, (512, 256))
        nq = T // tq
        qmap = lambda b, g, i: (b * nq + i, g)
        rows = NL
    else:
        tq, tk, nq = Tc, 0, 1
        qmap = lambda b, g, i: (cb + b, g)
        rows = B * Tc
    in_specs = [pl.BlockSpec((tq, qb), qmap),
                pl.BlockSpec((Tc, HEAD_DIM), lambda b, g, i: (cb + b, kcol + g)),
                pl.BlockSpec((ATT_VT_ROWS, Tc), lambda b, g, i: (g, cb + b))]
    args = [qk_att, qk_att, vt_att]
    if latent:
        in_specs += [pl.BlockSpec((T, HEAD_DIM), lambda b, g, i: (b, kcol + g)),
                     pl.BlockSpec((ATT_VT_ROWS, T), lambda b, g, i: (g, b))]
        args += [qk_att, vt_att]
    omap = (lambda b, g, i: (b * nq + i, g)) if latent else (lambda b, g, i: (b, g))
    return pl.pallas_call(
        functools.partial(_flash_body, tk, T // tk if latent else 0),
        grid=(B, GQA_KV_HEADS, nq),
        in_specs=in_specs,
        out_specs=pl.BlockSpec((tq, qb), omap),
        out_shape=jax.ShapeDtypeStruct((rows, GQA_WIDTH), BF16),
        scratch_shapes=[pltpu.VMEM((GQA_GROUP, HEAD_DIM, tq), BF16),
                        pltpu.VMEM((2, GQA_GROUP, max(tk, Tc), tq), F32),
                        pltpu.VMEM((2, max(tk, Tc), GQA_GROUP * tq), BF16),
                        pltpu.VMEM((GQA_GROUP, 1, tq), F32),
                        pltpu.VMEM((ATT_VT_ROWS, GQA_GROUP * tq), F32)],
        compiler_params=_cparams(("arbitrary", "arbitrary", "arbitrary")),
        name="attention_latent" if latent else "attention_context",
    )(*args)


def _layernorm(v, g, b):
    mu = jnp.mean(v, axis=-1, keepdims=True)
    c = v - mu
    var = jnp.mean(c * c, axis=-1, keepdims=True)
    return c * lax.rsqrt(var + LN_EPS) * g + b


def _ln1_body(alpha, y_ref, x_ref, m_ref, g_ref, b_ref, rw_ref, rb_ref, x1_ref, h2_ref, lg_ref):
    m = m_ref[0]
    x1 = _layernorm(alpha * x_ref[...] + m[2:3, :] * y_ref[...], g_ref[...], b_ref[...])
    x1_ref[...] = x1
    h2 = x1 * (1.0 + m[4:5, :]) + m[3:4, :]
    h2_ref[...] = h2
    lg_ref[...] = _dot3(h2, rw_ref[...]) + rb_ref[...]


def _ln1_router(y, xa, mod3, ln_g, ln_b, rw, rb, alpha, T, B):
    NA, D = xa.shape
    tr = _pick(NA, (256, 128))
    row = lambda i: (i, 0)
    fix = lambda i: (0, 0)
    return pl.pallas_call(
        functools.partial(_ln1_body, alpha),
        grid=(NA // tr,),
        in_specs=[pl.BlockSpec((tr, D), row), pl.BlockSpec((tr, D), row),
                  pl.BlockSpec((1, 6, D), _seg_map(tr, T, B)),
                  pl.BlockSpec((1, D), fix), pl.BlockSpec((1, D), fix),
                  pl.BlockSpec((D, LANES), fix), pl.BlockSpec((1, LANES), fix)],
        out_specs=[pl.BlockSpec((tr, D), row), pl.BlockSpec((tr, D), row), pl.BlockSpec((tr, LANES), row)],
        out_shape=[jax.ShapeDtypeStruct((NA, D), F32), jax.ShapeDtypeStruct((NA, D), F32),
                   jax.ShapeDtypeStruct((NA, LANES), F32)],
        compiler_params=_cparams(("arbitrary",)),
        name="ln1_router",
    )(y, xa, mod3, ln_g.reshape(1, D), ln_b.reshape(1, D), rw, rb)


def _row_copy(src_hbm, row, dst, j, sem):
    return pltpu.make_async_copy(src_hbm.at[pl.ds(row, 1)], dst.at[pl.ds(j, 1)], sem)


def _moe_gather_body(tm, src_ref, nv_ref, h_hbm, o_ref, buf, sem):
    i = pl.program_id(0)
    nv = nv_ref[0]
    slot = lax.rem(i, 2)

    def issue(tile, sl):
        def f(j, c):
            _row_copy(h_hbm, src_ref[tile * tm + j], buf.at[sl], j, sem.at[sl]).start()
            return c
        lax.fori_loop(0, tm, f, 0, unroll=DMA_UNROLL)

    def drain(tile, sl):
        def f(j, c):
            _row_copy(h_hbm, src_ref[tile * tm + j], buf.at[sl], j, sem.at[sl]).wait()
            return c
        lax.fori_loop(0, tm, f, 0, unroll=DMA_UNROLL)

    @pl.when((i == 0) & (nv > 0))
    def _():
        issue(0, 0)

    @pl.when(i + 1 < nv)
    def _():
        issue(i + 1, 1 - slot)

    @pl.when(i < nv)
    def _():
        drain(i, slot)
        o_ref[...] = buf[slot].astype(o_ref.dtype)

    @pl.when(i >= nv)
    def _():
        o_ref[...] = jnp.zeros_like(o_ref)


def _moe_gather(h2, row_src, nvalid, tm):
    D = h2.shape[1]
    P = row_src.shape[0]
    return pl.pallas_call(
        functools.partial(_moe_gather_body, tm),
        grid_spec=pltpu.PrefetchScalarGridSpec(
            num_scalar_prefetch=2,
            grid=(P // tm,),
            in_specs=[pl.BlockSpec(memory_space=pl.ANY)],
            out_specs=pl.BlockSpec((tm, D), lambda i, src, nv: (i, 0)),
            scratch_shapes=[pltpu.VMEM((2, tm, D), F32), pltpu.SemaphoreType.DMA((2,))]),
        out_shape=jax.ShapeDtypeStruct((P, D), BF16),
        compiler_params=_cparams(("arbitrary",)),
        name="moe_gather",
    )(row_src, nvalid, h2)


def _moe_ffn_body(te_ref, nv_ref, x_ref, w1_ref, w3_ref, w2_ref, o_ref, w1b, w3b, w2b):
    i = pl.program_id(0)
    used = i < nv_ref[0]
    new_expert = (i == 0) | (te_ref[i] != te_ref[jnp.maximum(i - 1, 0)])

    @pl.when(used & new_expert)
    def _():
        w1b[...] = w1_ref[0, 0].astype(BF16)
        w3b[...] = w3_ref[0, 0].astype(BF16)
        w2b[...] = w2_ref[0, 0].astype(BF16)

    @pl.when(used)
    def _():
        x = x_ref[...]
        a = _silu(_dot(x, w1b[...])) * _dot(x, w3b[...])
        o_ref[...] = _dot(a.astype(BF16), w2b[...])

    @pl.when(jnp.logical_not(used))
    def _():
        o_ref[...] = jnp.zeros_like(o_ref)


def _moe_ffn(xs, tile_expert, nvalid, w1, w3, w2, layer, tm):
    P, D = xs.shape
    F = w1.shape[3]
    wmap = lambda i, te, nv: (layer, te[i], 0, 0)
    once = pl.Buffered(1)
    return pl.pallas_call(
        _moe_ffn_body,
        grid_spec=pltpu.PrefetchScalarGridSpec(
            num_scalar_prefetch=2,
            grid=(P // tm,),
            in_specs=[pl.BlockSpec((tm, D), lambda i, te, nv: (i, 0)),
                      pl.BlockSpec((1, 1, D, F), wmap, pipeline_mode=once),
                      pl.BlockSpec((1, 1, D, F), wmap, pipeline_mode=once),
                      pl.BlockSpec((1, 1, F, D), wmap, pipeline_mode=once)],
            out_specs=pl.BlockSpec((tm, D), lambda i, te, nv: (i, 0)),
            scratch_shapes=[pltpu.VMEM((D, F), BF16), pltpu.VMEM((D, F), BF16), pltpu.VMEM((F, D), BF16)]),
        out_shape=jax.ShapeDtypeStruct((P, D), F32),
        compiler_params=_cparams(("arbitrary",)),
        name="moe_ffn",
    )(tile_expert, nvalid, xs, w1, w3, w2)


def _combine_body(alpha, tr, has_next, pos_ref, ys_hbm, x_ref, gt_ref, m_ref, g_ref, b_ref, *rest):
    if has_next:
        mn_ref, x2_ref, hn_ref, ya, yb, sem = rest
    else:
        x2_ref, ya, yb, sem = rest
    i = pl.program_id(0)
    slot = lax.rem(i, 2)

    def copies(tile, sl, j):
        r = 2 * (tile * tr + j)
        return (_row_copy(ys_hbm, pos_ref[r], ya.at[sl], j, sem.at[sl]),
                _row_copy(ys_hbm, pos_ref[r + 1], yb.at[sl], j, sem.at[sl]))

    def issue(tile, sl):
        def f(j, c):
            for cp in copies(tile, sl, j):
                cp.start()
            return c
        lax.fori_loop(0, tr, f, 0, unroll=DMA_UNROLL)

    @pl.when(i == 0)
    def _():
        issue(0, 0)

    @pl.when(i + 1 < pl.num_programs(0))
    def _():
        issue(i + 1, 1 - slot)

    def drain(j, c):
        for cp in copies(i, slot, j):
            cp.wait()
        return c
    lax.fori_loop(0, tr, drain, 0, unroll=DMA_UNROLL)
    gt = gt_ref[...]
    y = gt[:, 0:1] * ya[slot] + gt[:, 1:2] * yb[slot]
    m = m_ref[0]
    x2 = _layernorm(alpha * x_ref[...] + m[5:6, :] * y, g_ref[...], b_ref[...])
    x2_ref[...] = x2
    if has_next:
        mn = mn_ref[0]
        hn_ref[...] = (x2 * (1.0 + mn[1:2, :]) + mn[0:1, :]).astype(hn_ref.dtype)


def _moe_combine(ys, pos, gates, x1, mod3, ln_g, ln_b, mod3_next, alpha, T, B):
    has_next = mod3_next is not None
    D = x1.shape[1]
    NA = x1.shape[0] if has_next else B * T
    tr = _pick(NA, (256, 128))
    row = lambda i, p: (i, 0)
    fix = lambda i, p: (0, 0)
    seg = _seg_map(tr, T, B)
    in_specs = [pl.BlockSpec(memory_space=pl.ANY),
                pl.BlockSpec((tr, D), row), pl.BlockSpec((tr, LANES), row),
                pl.BlockSpec((1, 6, D), seg),
                pl.BlockSpec((1, D), fix), pl.BlockSpec((1, D), fix)]
    args = [pos, ys, x1, gates, mod3, ln_g.reshape(1, D), ln_b.reshape(1, D)]
    out_specs = [pl.BlockSpec((tr, D), row)]
    out_shape = [jax.ShapeDtypeStruct((NA, D), F32)]
    if has_next:
        in_specs.append(pl.BlockSpec((1, 6, D), seg))
        args.append(mod3_next)
        out_specs.append(pl.BlockSpec((tr, D), row))
        out_shape.append(jax.ShapeDtypeStruct((NA, D), BF16))
    return pl.pallas_call(
        functools.partial(_combine_body, alpha, tr, has_next),
        grid_spec=pltpu.PrefetchScalarGridSpec(
            num_scalar_prefetch=1,
            grid=(NA // tr,),
            in_specs=in_specs,
            out_specs=out_specs,
            scratch_shapes=[pltpu.VMEM((2, tr, D), F32), pltpu.VMEM((2, tr, D), F32),
                            pltpu.SemaphoreType.DMA((2,))]),
        out_shape=out_shape,
        compiler_params=_cparams(("arbitrary",)),
        name="moe_combine",
    )(*args)


def _route(logits):
    lg = logits[:, :MOE_GROUPS]
    le_all = logits[:, MOE_GROUPS:MOE_GROUPS + MOE_EXPERTS]
    n = lg.shape[0]
    grp = jnp.argmax(lg, -1)
    p_grp = jnp.take_along_axis(jax.nn.softmax(lg, -1), grp[:, None], axis=1)
    le = jnp.take_along_axis(le_all.reshape(n, MOE_GROUPS, MOE_EPG), grp[:, None, None], axis=1)[:, 0]
    top_v, top_i = lax.top_k(le, 2)
    w_sel = p_grp * jax.nn.softmax(top_v, -1)
    eid = grp[:, None] * MOE_EPG + top_i
    return eid.astype(jnp.int32), w_sel


def _dispatch(eid, tm):
    flat = eid.reshape(-1)
    na = flat.shape[0]
    experts = jnp.arange(MOE_EXPERTS, dtype=jnp.int32)
    order = jnp.argsort(flat, stable=True).astype(jnp.int32)
    inv = jnp.argsort(order).astype(jnp.int32)
    counts = jnp.sum((flat[:, None] == experts[None, :]).astype(jnp.int32), axis=0)
    starts = jnp.cumsum(counts) - counts
    ptiles = (counts + tm - 1) // tm
    pstart_t = jnp.cumsum(ptiles) - ptiles
    pos = (pstart_t[flat] * tm + (inv - starts[flat])).astype(jnp.int32)
    nt = na // tm + MOE_EXPERTS
    nvalid = jnp.sum(ptiles).astype(jnp.int32)
    tile_ids = jnp.arange(nt, dtype=jnp.int32)
    tile_expert = jnp.sum((tile_ids[:, None] >= (pstart_t + ptiles)[None, :]).astype(jnp.int32), axis=1)
    tile_expert = jnp.minimum(tile_expert, MOE_EXPERTS - 1)
    last_e = tile_expert[jnp.maximum(nvalid - 1, 0)]
    tile_expert = jnp.where(tile_ids < nvalid, tile_expert, last_e)
    prow = jnp.arange(nt * tm, dtype=jnp.int32)
    pe = jnp.repeat(tile_expert, tm)
    within = jnp.minimum(prow - pstart_t[pe] * tm, jnp.maximum(counts[pe] - 1, 0))
    row_src = order[jnp.clip(starts[pe] + within, 0, na - 1)] // 2
    return row_src.astype(jnp.int32), pos, tile_expert.astype(jnp.int32), nvalid.reshape(1)


def _rope_tables(T):
    rows = T // GRID_W
    row = jnp.repeat(jnp.arange(rows, dtype=F32), GRID_W)
    col = jnp.tile(jnp.arange(GRID_W, dtype=F32), rows)
    axis_dim = HEAD_DIM // 2
    inv = ROPE_THETA ** (-jnp.arange(0, axis_dim, 2, dtype=F32) / axis_dim)
    ang = jnp.concatenate([row[:, None] * inv, col[:, None] * inv], -1)
    c, s = jnp.cos(ang), jnp.sin(ang)
    return jnp.concatenate([c, c], -1), jnp.concatenate([-s, s], -1)


def kernel(x, c, ctx, c_ctx, w_mod, b_mod, w_in, gdn_conv, gdn_a_log, gdn_dt_bias, gdn_norm, fn_w,
           q_norm, k_norm, w_out, ln1_g, ln1_b, ln2_g, ln2_b, router_g, router_g_b, router_e,
           router_e_b, w1, w3, w2):
    B, T, D = x.shape
    Tc = ctx.shape[1]
    L = w_mod.shape[0]
    NL, NC = B * T, B * Tc
    alpha = (2 * L) ** 0.25
    H = GDN_HEADS

    xa = jnp.concatenate([x.reshape(NL, D), ctx.reshape(NC, D)], axis=0)
    sc = jax.nn.silu(jnp.concatenate([c, c_ctx[None, :]], axis=0))
    sc8 = jnp.pad(sc, ((0, SUBLANES - B - 1), (0, 0)))
    mod = _modulation(sc8, w_mod, b_mod)
    cs_tab, sn_tab = _rope_tables(T)
    w1st, m2, m2c, cs_dft = _fn_tables(T, Tc)
    T1, T1c = T // LANES, Tc // LANES
    tm_moe = 256

    h = None
    for l in range(L):
        last = l == L - 1
        mod3 = mod[l].reshape(SUBLANES, 6, D)
        if h is None:
            h = _modcast(xa, mod3, T, B)
        wl = w_in[l]
        o = np.cumsum([0, 3 * GDN_WIDTH, GDN_WIDTH, 2 * H, 2 * H, FN_WIDTH, GQA_WIDTH, GQA_KV_WIDTH,
                       GQA_KV_WIDTH])
        w_main = jnp.concatenate([wl[:, o[0]:o[2]], wl[:, o[4]:o[8]]], axis=1).astype(BF16)
        wa, wb = wl[:, o[2]:o[3]], wl[:, o[3]:o[4]]
        zpad = jnp.zeros((D, LANES - 2 * H), F32)
        w_ab = jnp.concatenate([wa[:, :H], wb[:, :H], zpad, wa[:, H:], wb[:, H:], zpad], axis=1).astype(BF16)
        p_main = _mm([h], [w_main], F32, "in_proj")
        p_ab = _mm([h], [w_ab], F32, "in_proj_gates")
        qkvn = _gdn_inputs(p_main, gdn_conv[l], B, T, Tc)
        prow = jnp.pad(jnp.stack([gdn_a_log[l], gdn_dt_bias[l]], axis=1),
                       ((0, 0), (0, SUBLANES - 2), (0, LANES - H)))
        pcol = jnp.swapaxes(prow, 1, 2)
        p_abt = jnp.swapaxes(p_ab.reshape(-1, GDN_CHUNK, 2 * LANES), 1, 2)
        o_fwd, o_bwd = _gdn_scan(*_gdn_chunks(qkvn, p_ab, p_abt, prow, pcol), B, T, Tc)
        gdn_y = _gdn_output(o_fwd, o_bwd, p_main, gdn_norm[l])
        f = p_main[:, OFF_F:OFF_F + FN_WIDTH]
        fnw = fn_w[l].astype(BF16)
        z = _fn_stage1(f[:NL].reshape(B, T1, LANES * FN_WIDTH), w1st)
        fn_l = _fn_stage2(z.reshape(B, 2, T1, LANES, FN_WIDTH), m2, cs_dft, fnw, same_z=False)
        fn_c = _fn_stage2(f[NL:].reshape(B, 2, 1, LANES, FN_WIDTH), m2c, cs_dft, fnw, same_z=True)
        fn_y = jnp.concatenate([fn_l.reshape(NL, FN_WIDTH), fn_c.reshape(NC, FN_WIDTH)], axis=0)
        qk_att, vt_att = _att_prep(p_main, cs_tab, sn_tab, q_norm[l], k_norm[l], B, T)
        at_y = jnp.concatenate([_flash(qk_att, vt_att, B, T, Tc, True),
                                _flash(qk_att, vt_att, B, T, Tc, False)], axis=0)
        wo = w_out[l]
        ws = [wo[:GDN_WIDTH].astype(BF16), wo[GDN_WIDTH:GDN_WIDTH + FN_WIDTH].astype(BF16),
              wo[GDN_WIDTH + FN_WIDTH:].astype(BF16)]
        y = _mm([gdn_y, fn_y, at_y], ws, F32, "out_proj")
        rw = jnp.concatenate([router_g[l], router_e[l],
                              jnp.zeros((D, LANES - MOE_GROUPS - MOE_EXPERTS), F32)], axis=1)
        rb = jnp.concatenate([router_g_b[l], router_e_b[l],
                              jnp.zeros((LANES - MOE_GROUPS - MOE_EXPERTS,), F32)])[None, :]
        x1, h2, logits = _ln1_router(y, xa, mod3, ln1_g[l], ln1_b[l], rw, rb, alpha, T, B)
        eid, w_sel = _route(logits)
        row_src, pos, tile_expert, nvalid = _dispatch(eid, tm_moe)
        xs = _moe_gather(h2, row_src, nvalid, tm_moe)
        ys = _moe_ffn(xs, tile_expert, nvalid, w1, w3, w2, l, tm_moe)
        gates = jnp.pad(w_sel, ((0, 0), (0, LANES - 2)))
        mod3_next = None if last else mod[l + 1].reshape(SUBLANES, 6, D)
        res = _moe_combine(ys, pos, gates, x1, mod3, ln2_g[l], ln2_b[l], mod3_next, alpha, T, B)
        xa = res[0]
        h = None if last else res[1]
    return xa.reshape(B, T, D)
```

```python
import functools
import math

import numpy as np
import jax
import jax.numpy as jnp
from jax import lax
from jax.experimental import pallas as pl
from jax.experimental.pallas import tpu as pltpu

F32 = jnp.float32
BF16 = jnp.bfloat16

HEAD_DIM = 128
GDN_HEADS = 12
GDN_WIDTH = GDN_HEADS * HEAD_DIM
GDN_CONV_W = 5
GDN_CHUNK = 64
GDN_INV_BLOCK = 8
FN_GROUPS = 8
FN_WIDTH = FN_GROUPS * HEAD_DIM
GQA_HEADS = 12
GQA_KV_HEADS = 4
GQA_GROUP = GQA_HEADS // GQA_KV_HEADS
GQA_WIDTH = GQA_HEADS * HEAD_DIM
GQA_KV_WIDTH = GQA_KV_HEADS * HEAD_DIM
GRID_W = 64
ROPE_THETA = 10000.0
MOE_GROUPS = 4
MOE_EPG = 8
MOE_EXPERTS = MOE_GROUPS * MOE_EPG
LN_EPS = 1e-5
RMS_EPS = 1e-6
LANES = 128
SUBLANES = 8
VMEM_LIMIT = 56 * 1024 * 1024
DMA_UNROLL = 8
ATT_ROW_BLOCK = 64
ATT_VT_ROWS = HEAD_DIM + 16

OFF_QKV = 0
OFF_Z = 3 * GDN_WIDTH
OFF_F = OFF_Z + GDN_WIDTH
OFF_Q = OFF_F + FN_WIDTH
OFF_K = OFF_Q + GQA_WIDTH
OFF_V = OFF_K + GQA_KV_WIDTH
N_MAIN = OFF_V + GQA_KV_WIDTH


def _cparams(sem, **kw):
    return pltpu.CompilerParams(dimension_semantics=sem, vmem_limit_bytes=VMEM_LIMIT, **kw)


def _pick(n, cands):
    for c in cands:
        if n % c == 0:
            return c
    raise ValueError(f"no tile for {n} in {cands}")


def _dot(a, b):
    return jnp.dot(a, b, preferred_element_type=F32)


def _dot_nt(a, b):
    return lax.dot_general(a, b, (((1,), (1,)), ((), ())), preferred_element_type=F32)


def _split2(a):
    hi = a.astype(BF16)
    lo = (a - hi.astype(F32)).astype(BF16)
    return hi, lo


def _dot3(a, b):
    ah, al = _split2(a)
    bh, bl = _split2(b)
    return _dot(ah, bh) + (_dot(ah, bl) + _dot(al, bh))


def _silu(x):
    return x / (1.0 + jnp.exp(-x))


def _mod_body(x_ref, w_ref, b_ref, o_ref):
    o_ref[0] = _dot3(x_ref[...], w_ref[0]) + b_ref[0]


def _modulation(sc8, w_mod, b_mod):
    L, D, N = w_mod.shape
    tn = _pick(N, (512, 256, 128))
    return pl.pallas_call(
        _mod_body,
        grid=(L, N // tn),
        in_specs=[pl.BlockSpec((SUBLANES, D), lambda l, j: (0, 0)),
                  pl.BlockSpec((1, D, tn), lambda l, j: (l, 0, j)),
                  pl.BlockSpec((1, 1, tn), lambda l, j: (l, 0, j))],
        out_specs=pl.BlockSpec((1, SUBLANES, tn), lambda l, j: (l, 0, j)),
        out_shape=jax.ShapeDtypeStruct((L, SUBLANES, N), F32),
        compiler_params=_cparams(("arbitrary", "arbitrary")),
        name="modulation",
    )(sc8, w_mod, b_mod.reshape(L, 1, N))


def _seg_map(tr, T, B):
    return lambda i, *_: (jnp.minimum((i * tr) // T, B), 0, 0)


def _modcast_body(x_ref, m_ref, o_ref):
    m = m_ref[0]
    o_ref[...] = (x_ref[...] * (1.0 + m[1:2, :]) + m[0:1, :]).astype(o_ref.dtype)


def _modcast(xa, mod3, T, B):
    NA, D = xa.shape
    tr = _pick(NA, (256, 128))
    return pl.pallas_call(
        _modcast_body,
        grid=(NA // tr,),
        in_specs=[pl.BlockSpec((tr, D), lambda i: (i, 0)),
                  pl.BlockSpec((1, 6, D), _seg_map(tr, T, B))],
        out_specs=pl.BlockSpec((tr, D), lambda i: (i, 0)),
        out_shape=jax.ShapeDtypeStruct((NA, D), BF16),
        compiler_params=_cparams(("arbitrary",)),
        name="modcast",
    )(xa, mod3)


def _mm_body(n, out_dtype, *refs):
    xs, ws, o_ref = refs[:n], refs[n:2 * n], refs[2 * n]
    acc = _dot(xs[0][...], ws[0][...])
    for x_ref, w_ref in zip(xs[1:], ws[1:]):
        acc = acc + _dot(x_ref[...], w_ref[...])
    o_ref[...] = acc.astype(out_dtype)


def _mm(xs, ws, out_dtype, name):
    M = xs[0].shape[0]
    N = ws[0].shape[1]
    tm = _pick(M, (1536, 1024, 768, 512, 256))
    tn = _pick(N, (512, 256, 128))
    n = len(xs)
    in_specs = ([pl.BlockSpec((tm, x.shape[1]), lambda i, j: (i, 0)) for x in xs]
                + [pl.BlockSpec((w.shape[0], tn), lambda i, j: (0, j)) for w in ws])
    return pl.pallas_call(
        functools.partial(_mm_body, n, out_dtype),
        grid=(M // tm, N // tn),
        in_specs=in_specs,
        out_specs=pl.BlockSpec((tm, tn), lambda i, j: (i, j)),
        out_shape=jax.ShapeDtypeStruct((M, N), out_dtype),
        compiler_params=_cparams(("arbitrary", "arbitrary")),
        name=name,
    )(*xs, *ws)


def _gdn_in_body(B, T, Tc, tr, prev_ref, cur_ref, next_ref, w_ref, o_ref, buf):
    i = pl.program_id(0)
    j = pl.program_id(1)
    row0 = i * tr
    nl = B * T
    is_lat = row0 < nl
    r = jnp.where(is_lat, lax.rem(row0, T), lax.rem(row0 - nl, Tc))
    seq = jnp.where(is_lat, T, Tc)
    at_start = r == 0
    at_end = r + tr == seq
    buf[0:SUBLANES, :] = jnp.where(at_start, 0.0, prev_ref[...])
    buf[SUBLANES:SUBLANES + tr, :] = cur_ref[...]
    buf[SUBLANES + tr:2 * SUBLANES + tr, :] = jnp.where(at_end, 0.0, next_ref[...])
    pad = GDN_CONV_W // 2
    acc = buf[pl.ds(SUBLANES - pad, tr), :] * w_ref[0:1, :]
    for t in range(1, GDN_CONV_W):
        acc = acc + buf[pl.ds(SUBLANES - pad + t, tr), :] * w_ref[t:t + 1, :]
    act = _silu(acc)
    scale = jnp.where(j == 0, HEAD_DIM ** -0.5, 1.0)
    do_norm = j < 2
    for h in range(GDN_HEADS):
        a = act[:, h * HEAD_DIM:(h + 1) * HEAD_DIM]
        nrm = a * lax.rsqrt(jnp.sum(a * a, axis=-1, keepdims=True) + RMS_EPS) * scale
        o_ref[:, h * HEAD_DIM:(h + 1) * HEAD_DIM] = jnp.where(do_norm, nrm, a)


def _gdn_inputs(p_main, conv_w, B, T, Tc):
    NA = p_main.shape[0]
    tr = 256
    assert T % tr == 0 and Tc % tr == 0
    nb8 = NA // SUBLANES
    r8 = tr // SUBLANES
    W = GDN_WIDTH
    return pl.pallas_call(
        functools.partial(_gdn_in_body, B, T, Tc, tr),
        grid=(NA // tr, 3),
        in_specs=[pl.BlockSpec((SUBLANES, W), lambda i, j: (jnp.maximum(i * r8 - 1, 0), j)),
                  pl.BlockSpec((tr, W), lambda i, j: (i, j)),
                  pl.BlockSpec((SUBLANES, W), lambda i, j: (jnp.minimum((i + 1) * r8, nb8 - 1), j)),
                  pl.BlockSpec((GDN_CONV_W, W), lambda i, j: (0, j))],
        out_specs=pl.BlockSpec((tr, W), lambda i, j: (i, j)),
        out_shape=jax.ShapeDtypeStruct((NA, 3 * W), F32),
        scratch_shapes=[pltpu.VMEM((tr + 2 * SUBLANES, W), F32)],
        compiler_params=_cparams(("arbitrary", "arbitrary")),
        name="gdn_inputs",
    )(p_main, p_main, p_main, conv_w)


def _softplus(x):
    return jnp.maximum(x, 0.0) + jnp.log1p(jnp.exp(-jnp.abs(x)))


def _sigmoid(x):
    return 1.0 / (1.0 + jnp.exp(-x))


def _split3(a):
    hi = a.astype(BF16)
    r1 = a - hi.astype(F32)
    mid = r1.astype(BF16)
    lo = (r1 - mid.astype(F32)).astype(BF16)
    return hi, mid, lo


def _gdn_chunk_body(q_ref, k_ref, v_ref, ab_ref, abt_ref, prow_ref, pcol_ref,
                    u_ref, wq_ref, qkd_ref, kdt_ref, eg_ref):
    C = GDN_CHUNK
    H = GDN_HEADS
    heads = range(H)
    ii = lax.broadcasted_iota(jnp.int32, (C, C), 0)
    jj = lax.broadcasted_iota(jnp.int32, (C, C), 1)
    eye = jnp.where(ii == jj, 1.0, 0.0)
    bs = GDN_INV_BLOCK
    diag_blk = (ii // bs) == (jj // bs)
    pair_off = []
    while bs < C:
        pair_off.append(((ii // bs) ^ (jj // bs)) == 1)
        bs *= 2

    sl = [slice(h * HEAD_DIM, (h + 1) * HEAD_DIM) for h in heads]
    kb = [k_ref[:, sl[h]].astype(BF16) for h in heads]
    kk = [_dot_nt(kb[h], kb[h]) for h in heads]
    qk = [_dot_nt(q_ref[:, sl[h]].astype(BF16), kb[h]) for h in heads]

    dh = [(d, h) for d in range(2) for h in heads]
    incl, strict, beta, gc, gct, gtot = {}, {}, {}, {}, {}, {}
    for d in range(2):
        ahead = (ii - jj) if d == 0 else (jj - ii)
        incl[d] = ahead >= 0
        strict[d] = ahead > 0
        tri = jnp.where(incl[d], 1.0, 0.0).astype(BF16)
        tri_t = jnp.where(ahead <= 0, 1.0, 0.0).astype(BF16)
        ab = ab_ref[:, d * LANES:(d + 1) * LANES]
        abt = abt_ref[0, d * LANES:(d + 1) * LANES, :]
        prow = prow_ref[d]
        pcol = pcol_ref[d]
        g = -jnp.exp(prow[0:1, :]) * _softplus(ab + prow[1:2, :])
        gt = -jnp.exp(pcol[:, 0:1]) * _softplus(abt + pcol[:, 1:2])
        beta[d] = _sigmoid(ab)
        g1, g2, g3 = _split3(g)
        gc[d] = _dot(tri, g1) + (_dot(tri, g2) + _dot(tri, g3))
        t1, t2, t3 = _split3(gt)
        gct[d] = _dot(t1, tri_t) + (_dot(t2, tri_t) + _dot(t3, tri_t))
        gtot[d] = jnp.sum(g, axis=0, keepdims=True)
        eg_ref[d, 0] = jnp.exp(gtot[d])

    bcol = {x: jnp.broadcast_to(beta[x[0]][:, H + x[1]:H + x[1] + 1], (C, HEAD_DIM)) for x in dh}
    gcol = {x: jnp.broadcast_to(gc[x[0]][:, x[1]:x[1] + 1], (C, HEAD_DIM)) for x in dh}
    decay = {(d, h): jnp.where(incl[d], jnp.exp(jnp.minimum(gcol[(d, h)][:, :C] - gct[d][h:h + 1, :], 0.0)), 0.0)
             for d, h in dh}
    a_mat = {(d, h): jnp.where(strict[d], kk[h] * decay[(d, h)] * bcol[(d, h)][:, :C], 0.0) for d, h in dh}
    for d, h in dh:
        qkd_ref[d, :, h * C:(h + 1) * C] = (qk[h] * decay[(d, h)]).astype(qkd_ref.dtype)
    p = {x: jnp.where(diag_blk, -a_mat[x], 0.0) for x in dh}
    tm = {x: eye + p[x] for x in dh}
    for _ in range(int(math.log2(GDN_INV_BLOCK)) - 1):
        pb = {x: p[x].astype(BF16) for x in dh}
        p = {x: _dot(pb[x], pb[x]) for x in dh}
        tm = {x: tm[x] + _dot(tm[x].astype(BF16), p[x].astype(BF16)) for x in dh}
    for off in pair_off:
        tb = {x: tm[x].astype(BF16) for x in dh}
        at = {x: _dot(jnp.where(off, a_mat[x], 0.0).astype(BF16), tb[x]) for x in dh}
        tm = {x: tm[x] - _dot(tb[x], at[x].astype(BF16)) for x in dh}
    eg = {x: jnp.exp(gcol[x]) for x in dh}
    k = [k_ref[:, sl[h]] for h in heads]
    rhs = {(d, h): jnp.concatenate([v_ref[:, sl[h]] * bcol[(d, h)], k[h] * (bcol[(d, h)] * eg[(d, h)])],
                                   axis=1).astype(BF16) for d, h in dh}
    uw = {x: _dot(tm[x].astype(BF16), rhs[x]) for x in dh}
    for d, h in dh:
        x = (d, h)
        u_ref[d, :, sl[h]] = uw[x][:, :HEAD_DIM]
        wq_ref[d, 0, h, :C, :] = uw[x][:, HEAD_DIM:].astype(wq_ref.dtype)
        wq_ref[d, 0, h, C:, :] = (q_ref[:, sl[h]] * eg[x]).astype(wq_ref.dtype)
        kdec = k[h] * jnp.exp(gtot[d][:, h:h + 1] - gcol[x])
        kdt_ref[d, 0, h] = kdec.T.astype(kdt_ref.dtype)


def _gdn_chunks(qkvn, p_ab, p_abt, prow, pcol):
    NA = qkvn.shape[0]
    C = GDN_CHUNK
    H = GDN_HEADS
    W = GDN_WIDTH
    n = NA // C
    return pl.pallas_call(
        _gdn_chunk_body,
        grid=(n,),
        in_specs=[pl.BlockSpec((C, W), lambda i: (i, 0)),
                  pl.BlockSpec((C, W), lambda i: (i, 1)),
                  pl.BlockSpec((C, W), lambda i: (i, 2)),
                  pl.BlockSpec((C, 2 * LANES), lambda i: (i, 0)),
                  pl.BlockSpec((1, 2 * LANES, C), lambda i: (i, 0, 0)),
                  pl.BlockSpec((2, SUBLANES, LANES), lambda i: (0, 0, 0)),
                  pl.BlockSpec((2, LANES, SUBLANES), lambda i: (0, 0, 0))],
        out_specs=[pl.BlockSpec((2, C, W), lambda i: (0, i, 0)),
                   pl.BlockSpec((2, 1, H, 2 * C, HEAD_DIM), lambda i: (0, i, 0, 0, 0)),
                   pl.BlockSpec((2, C, H * C), lambda i: (0, i, 0)),
                   pl.BlockSpec((2, 1, H, HEAD_DIM, C), lambda i: (0, i, 0, 0, 0)),
                   pl.BlockSpec((2, 1, 1, LANES), lambda i: (0, i, 0, 0))],
        out_shape=[jax.ShapeDtypeStruct((2, NA, W), F32),
                   jax.ShapeDtypeStruct((2, n, H, 2 * C, HEAD_DIM), BF16),
                   jax.ShapeDtypeStruct((2, NA, H * C), BF16),
                   jax.ShapeDtypeStruct((2, n, H, HEAD_DIM, C), BF16),
                   jax.ShapeDtypeStruct((2, n, 1, LANES), F32)],
        compiler_params=_cparams(("arbitrary",)),
        name="gdn_chunks",
    )(qkvn, qkvn, qkvn, p_ab, p_abt, prow, pcol)


def _gdn_scan_body(*refs):
    C = GDN_CHUNK
    H = GDN_HEADS
    ins, (o0_ref, o1_ref, s_ref) = refs[:10], refs[10:]
    o_refs = (o0_ref, o1_ref)

    @pl.when(pl.program_id(1) == 0)
    def _():
        s_ref[...] = jnp.zeros_like(s_ref)

    dh = [(d, h) for d in range(2) for h in range(H)]
    u_ref, wq_ref, qkd_ref, kdt_ref, eg_ref = [ins[2 * j:2 * j + 2] for j in range(5)]
    st = {x: s_ref[x[0], x[1]] for x in dh}
    wqs = {(d, h): _dot(wq_ref[d][0, 0, h], st[(d, h)].astype(BF16)) for d, h in dh}
    vnb = {}
    for d, h in dh:
        v_new = u_ref[d][0, :, h * HEAD_DIM:(h + 1) * HEAD_DIM] - wqs[(d, h)][:C]
        vnb[(d, h)] = v_new.astype(BF16)
    for d, h in dh:
        o = wqs[(d, h)][C:] + _dot(qkd_ref[d][0, :, h * C:(h + 1) * C], vnb[(d, h)])
        o_refs[d][:, h * HEAD_DIM:(h + 1) * HEAD_DIM] = o
    for d, h in dh:
        e = eg_ref[d][0, 0][:, h:h + 1]
        s_ref[d, h] = st[(d, h)] * e + _dot(kdt_ref[d][0, 0, h], vnb[(d, h)])


def _gdn_scan(u, wq, qkd, kdt, eg, B, T, Tc):
    NA = u.shape[1]
    C = GDN_CHUNK
    H = GDN_HEADS
    nl, nc = T // C, Tc // C
    W = GDN_WIDTH

    def rb(d):
        def f(b, s):
            ctx_blk = B * nl + b * nc + (s if d == 0 else nc - 1 - s)
            lat_blk = b * nl + ((s - nc) if d == 0 else nl - 1 - (s - nc))
            return jnp.where(s < nc, ctx_blk, lat_blk)
        return f

    in_specs, args = [], []
    for arr, blk, imap in (
            (u, (1, C, W), lambda d: (lambda b, s: (d, rb(d)(b, s), 0))),
            (wq, (1, 1, H, 2 * C, HEAD_DIM), lambda d: (lambda b, s: (d, rb(d)(b, s), 0, 0, 0))),
            (qkd, (1, C, H * C), lambda d: (lambda b, s: (d, rb(d)(b, s), 0))),
            (kdt, (1, 1, H, HEAD_DIM, C), lambda d: (lambda b, s: (d, rb(d)(b, s), 0, 0, 0))),
            (eg, (1, 1, 1, LANES), lambda d: (lambda b, s: (d, rb(d)(b, s), 0, 0)))):
        for d in range(2):
            in_specs.append(pl.BlockSpec(blk, imap(d)))
            args.append(arr)
    return pl.pallas_call(
        _gdn_scan_body,
        grid=(B, nc + nl),
        in_specs=in_specs,
        out_specs=[pl.BlockSpec((C, W), lambda b, s: (rb(0)(b, s), 0)),
                   pl.BlockSpec((C, W), lambda b, s: (rb(1)(b, s), 0))],
        out_shape=[jax.ShapeDtypeStruct((NA, W), F32), jax.ShapeDtypeStruct((NA, W), F32)],
        scratch_shapes=[pltpu.VMEM((2, H, HEAD_DIM, HEAD_DIM), F32)],
        compiler_params=_cparams(("arbitrary", "arbitrary")),
        name="gdn_scan",
    )(*args)


def _gdn_out_body(o0_ref, o1_ref, z_ref, w_ref, y_ref):
    o = o0_ref[...] + o1_ref[...]
    z = z_ref[...]
    w = w_ref[...]
    for h in range(GDN_HEADS):
        lo, hi = h * HEAD_DIM, (h + 1) * HEAD_DIM
        a = o[:, lo:hi]
        n = a * lax.rsqrt(jnp.mean(a * a, axis=-1, keepdims=True) + RMS_EPS) * w
        y_ref[:, lo:hi] = (n * _silu(z[:, lo:hi])).astype(y_ref.dtype)


def _gdn_output(o_fwd, o_bwd, p_main, norm_w):
    NA = p_main.shape[0]
    tr = 256
    W = GDN_WIDTH
    return pl.pallas_call(
        _gdn_out_body,
        grid=(NA // tr,),
        in_specs=[pl.BlockSpec((tr, W), lambda i: (i, 0)),
                  pl.BlockSpec((tr, W), lambda i: (i, 0)),
                  pl.BlockSpec((tr, W), lambda i: (i, OFF_Z // W)),
                  pl.BlockSpec((1, HEAD_DIM), lambda i: (0, 0))],
        out_specs=pl.BlockSpec((tr, W), lambda i: (i, 0)),
        out_shape=jax.ShapeDtypeStruct((NA, W), BF16),
        compiler_params=_cparams(("arbitrary",)),
        name="gdn_output",
    )(o_fwd, o_bwd, p_main, norm_w.reshape(1, HEAD_DIM))


def _fn_stage1_body(w_ref, x_ref, z_ref):
    t1 = x_ref.shape[1]
    z = _dot3(w_ref[...], x_ref[0])
    z_ref[0] = z.reshape(2, t1, z.shape[-1])


def _fn_stage1(xf, w1st):
    B, T1, NN = xf.shape
    tn = _pick(NN, (8192, 4096, 1024))
    return pl.pallas_call(
        _fn_stage1_body,
        grid=(B, NN // tn),
        in_specs=[pl.BlockSpec((2 * T1, T1), lambda b, j: (0, 0)),
                  pl.BlockSpec((1, T1, tn), lambda b, j: (b, 0, j))],
        out_specs=pl.BlockSpec((1, 2, T1, tn), lambda b, j: (b, 0, 0, j)),
        out_shape=jax.ShapeDtypeStruct((B, 2, T1, NN), F32),
        compiler_params=_cparams(("arbitrary", "arbitrary")),
        name="fnet_stage1",
    )(w1st, xf)


def _fn_stage2_body(m_ref, z_ref, cs_ref, fw_ref, o_ref):
    z = z_ref[0, :, 0].reshape(2 * LANES, FN_WIDTH)
    hh = _dot3(m_ref[0], z)
    hr, hi = hh[:LANES], hh[LANES:]
    cs = cs_ref[...]
    cols = []
    for g in range(FN_GROUPS):
        lo, up = g * HEAD_DIM, (g + 1) * HEAD_DIM
        cols.append(_dot3(jnp.concatenate([hr[:, lo:up], hi[:, lo:up]], axis=1), cs))
    fr = jnp.concatenate(cols, axis=1)
    o_ref[0] = _dot(fr.astype(BF16), fw_ref[...]).astype(o_ref.dtype)


def _fn_stage2(z5, m2, cs, fnw, same_z):
    B = z5.shape[0]
    T1 = m2.shape[0]
    zmap = (lambda b, k: (b, 0, 0, 0, 0)) if same_z else (lambda b, k: (b, 0, k, 0, 0))
    return pl.pallas_call(
        _fn_stage2_body,
        grid=(B, T1),
        in_specs=[pl.BlockSpec((1, 2 * LANES, 2 * LANES), lambda b, k: (k, 0, 0)),
                  pl.BlockSpec((1, 2, 1, LANES, FN_WIDTH), zmap),
                  pl.BlockSpec((2 * HEAD_DIM, HEAD_DIM), lambda b, k: (0, 0)),
                  pl.BlockSpec((FN_WIDTH, FN_WIDTH), lambda b, k: (0, 0))],
        out_specs=pl.BlockSpec((1, LANES, FN_WIDTH), lambda b, k: (b, 0, k)),
        out_shape=jax.ShapeDtypeStruct((B, LANES, T1 * FN_WIDTH), BF16),
        compiler_params=_cparams(("arbitrary", "arbitrary")),
        name="fnet_stage2",
    )(m2, z5, cs, fnw)


def _phase(num, den):
    ang = (2.0 * math.pi / den) * lax.rem(num, den).astype(F32)
    return jnp.cos(ang), jnp.sin(ang)


def _fn_tables(T, Tc):
    t1 = T // LANES
    a = jnp.arange(t1, dtype=jnp.int32)
    c1, s1 = _phase(a[:, None] * a[None, :], t1)
    w1st = jnp.concatenate([c1, -s1], axis=0)
    k1 = jnp.arange(t1, dtype=jnp.int32)[:, None, None]
    k2 = jnp.arange(LANES, dtype=jnp.int32)[None, :, None]
    t2 = jnp.arange(LANES, dtype=jnp.int32)[None, None, :]
    c, s = _phase(k2 * t2 * t1 + k1 * t2, T)
    sc = (T * HEAD_DIM) ** -0.5
    m2 = jnp.concatenate([jnp.concatenate([c, s], axis=2), jnp.concatenate([-s, c], axis=2)], axis=1) * sc
    t1c = Tc // LANES
    k = (jnp.arange(t1c, dtype=jnp.int32)[:, None, None]
         + t1c * jnp.arange(LANES, dtype=jnp.int32)[None, :, None])
    t = jnp.arange(Tc, dtype=jnp.int32)[None, None, :]
    cc, sc_ = _phase(k * t, Tc)
    m2c = jnp.concatenate([cc, -sc_], axis=1) * (Tc * HEAD_DIM) ** -0.5
    ch = jnp.arange(HEAD_DIM, dtype=jnp.int32)
    c3, s3 = _phase(ch[:, None] * ch[None, :], HEAD_DIM)
    cs = jnp.concatenate([c3, s3], axis=0)
    return w1st, m2, m2c, cs


def _att_prep_body(B, T, tr, x_ref, cs_ref, sn_ref, qw_ref, kw_ref, o_ref, vt_ref):
    i = pl.program_id(0)
    j = pl.program_id(1)
    nh = x_ref.shape[1] // HEAD_DIM

    @pl.when(j == 4)
    def _():
        vt = x_ref[...].T.astype(vt_ref.dtype)
        for h in range(nh):
            vt_ref[h * ATT_VT_ROWS:h * ATT_VT_ROWS + HEAD_DIM, :] = vt[h * HEAD_DIM:(h + 1) * HEAD_DIM]
            vt_ref[h * ATT_VT_ROWS + HEAD_DIM:(h + 1) * ATT_VT_ROWS, :] = jnp.ones(
                (ATT_VT_ROWS - HEAD_DIM, vt.shape[1]), vt_ref.dtype)

    @pl.when(j < 4)
    def _():
        is_lat = i * tr < B * T
        is_q = j < 3
        w = jnp.where(is_q, qw_ref[...], kw_ref[...])
        scale = jnp.where(is_q, HEAD_DIM ** -0.5 * math.log2(math.e), 1.0)
        cs = cs_ref[...]
        sn = sn_ref[...]
        for h in range(nh):
            lo, hi = h * HEAD_DIM, (h + 1) * HEAD_DIM
            a = x_ref[:, lo:hi]
            n = a * lax.rsqrt(jnp.mean(a * a, axis=-1, keepdims=True) + RMS_EPS) * w
            rot = n * cs + pltpu.roll(n, HEAD_DIM // 2, 1) * sn
            o_ref[:, lo:hi] = (jnp.where(is_lat, rot, n) * scale).astype(o_ref.dtype)


def _att_prep(p_main, cs_tab, sn_tab, qw, kw, B, T):
    NA = p_main.shape[0]
    tr = 256
    cw = GQA_KV_WIDTH
    base = OFF_Q // cw
    nrt = T // tr
    return pl.pallas_call(
        functools.partial(_att_prep_body, B, T, tr),
        grid=(NA // tr, 5),
        in_specs=[pl.BlockSpec((tr, cw), lambda i, j: (i, base + j)),
                  pl.BlockSpec((tr, HEAD_DIM), lambda i, j: (lax.rem(i, nrt), 0)),
                  pl.BlockSpec((tr, HEAD_DIM), lambda i, j: (lax.rem(i, nrt), 0)),
                  pl.BlockSpec((1, HEAD_DIM), lambda i, j: (0, 0)),
                  pl.BlockSpec((1, HEAD_DIM), lambda i, j: (0, 0))],
        out_specs=[pl.BlockSpec((tr, cw), lambda i, j: (i, jnp.minimum(j, 3))),
                   pl.BlockSpec((GQA_KV_HEADS * ATT_VT_ROWS, tr), lambda i, j: (0, i))],
        out_shape=[jax.ShapeDtypeStruct((NA, 4 * cw), BF16),
                   jax.ShapeDtypeStruct((GQA_KV_HEADS * ATT_VT_ROWS, NA), BF16)],
        compiler_params=_cparams(("arbitrary", "arbitrary")),
        name="attention_prep",
    )(p_main, cs_tab, sn_tab, qw.reshape(1, HEAD_DIM), kw.reshape(1, HEAD_DIM))


def _flash_body(tk, n_lat, *refs):
    if n_lat:
        q_ref, kc_ref, vtc_ref, kl_ref, vtl_ref, o_ref, qt_sc, s_sc, p_sc, m_sc, acc_sc = refs
    else:
        q_ref, kc_ref, vtc_ref, o_ref, qt_sc, s_sc, p_sc, m_sc, acc_sc = refs
    heads = range(GQA_GROUP)
    tq = q_ref.shape[0]
    for g in heads:
        qt_sc[g] = q_ref[:, g * HEAD_DIM:(g + 1) * HEAD_DIM].astype(F32).T.astype(BF16)
    m_sc[...] = jnp.full_like(m_sc, -jnp.inf)
    acc_sc[...] = jnp.zeros_like(acc_sc)

    def scores(slot, kc):
        for g in heads:
            s_sc[slot, g, :kc.shape[0], :] = _dot(kc, qt_sc[g])

    def softmax_pv(slot, vtc):
        nk = vtc.shape[1]
        blocks = [slice(r, r + ATT_ROW_BLOCK) for r in range(0, nk, ATT_ROW_BLOCK)]
        alphas = []
        for g in heads:
            cols = slice(g * tq, (g + 1) * tq)
            mx = s_sc[slot, g, blocks[0], :]
            for rb in blocks[1:]:
                mx = jnp.maximum(mx, s_sc[slot, g, rb, :])
            m_old = m_sc[g]
            m_new = jnp.maximum(m_old, jnp.max(mx, axis=0, keepdims=True))
            alphas.append(jnp.exp2(m_old - m_new))
            for rb in blocks:
                p_sc[slot, rb, cols] = jnp.exp2((s_sc[slot, g, rb, :] - m_new).astype(BF16))
            m_sc[g] = m_new
        acc_sc[...] = jnp.concatenate(alphas, axis=1) * acc_sc[...] + _dot(vtc, p_sc[slot, :nk, :])

    scores(0, kc_ref[...])
    softmax_pv(0, vtc_ref[...])
    if n_lat:
        def pair(c, carry):
            offs = [pl.multiple_of((2 * c + j) * tk, tk) for j in range(2)]
            for j in range(2):
                scores(j, kl_ref[pl.ds(offs[j], tk), :])
            for j in range(2):
                softmax_pv(j, vtl_ref[:, pl.ds(offs[j], tk)])
            return carry
        lax.fori_loop(0, n_lat // 2, pair, 0)
        if n_lat % 2:
            last = (n_lat - 1) * tk
            scores(0, kl_ref[last:last + tk, :])
            softmax_pv(0, vtl_ref[:, last:last + tk])
    for g in heads:
        cols = slice(g * tq, (g + 1) * tq)
        o = acc_sc[:HEAD_DIM, cols] / acc_sc[HEAD_DIM:HEAD_DIM + 1, cols]
        o_ref[:, g * HEAD_DIM:(g + 1) * HEAD_DIM] = o.T.astype(o_ref.dtype)


def _flash(qk_att, vt_att, B, T, Tc, latent):
    NL = B * T
    qb = GQA_GROUP * HEAD_DIM
    kcol = GQA_WIDTH // HEAD_DIM
    cb = NL // Tc
    if latent:
        tq = _pick(T, (512, 256, 128))
        tk = _pick(T, (1024, 512, 256))
        nq = T // tq
        qmap = lambda b, g, i: (b * nq + i, g)
        rows = NL
    else:
        tq, tk, nq = Tc, 0, 1
        qmap = lambda b, g, i: (cb + b, g)
        rows = B * Tc
    in_specs = [pl.BlockSpec((tq, qb), qmap),
                pl.BlockSpec((Tc, HEAD_DIM), lambda b, g, i: (cb + b, kcol + g)),
                pl.BlockSpec((ATT_VT_ROWS, Tc), lambda b, g, i: (g, cb + b))]
    args = [qk_att, qk_att, vt_att]
    if latent:
        in_specs += [pl.BlockSpec((T, HEAD_DIM), lambda b, g, i: (b, kcol + g)),
                     pl.BlockSpec((ATT_VT_ROWS, T), lambda b, g, i: (g, b))]
        args += [qk_att, vt_att]
    omap = (lambda b, g, i: (b * nq + i, g)) if latent else (lambda b, g, i: (b, g))
    return pl.pallas_call(
        functools.partial(_flash_body, tk, T // tk if latent else 0),
        grid=(B, GQA_KV_HEADS, nq),
        in_specs=in_specs,
        out_specs=pl.BlockSpec((tq, qb), omap),
        out_shape=jax.ShapeDtypeStruct((rows, GQA_WIDTH), BF16),
        scratch_shapes=[pltpu.VMEM((GQA_GROUP, HEAD_DIM, tq), BF16),
                        pltpu.VMEM((2, GQA_GROUP, max(tk, Tc), tq), F32),
                        pltpu.VMEM((2, max(tk, Tc), GQA_GROUP * tq), BF16),
                        pltpu.VMEM((GQA_GROUP, 1, tq), F32),
                        pltpu.VMEM((ATT_VT_ROWS, GQA_GROUP * tq), F32)],
        compiler_params=_cparams(("arbitrary", "arbitrary", "arbitrary")),
        name="attention_latent" if latent else "attention_context",
    )(*args)


def _layernorm(v, g, b):
    mu = jnp.mean(v, axis=-1, keepdims=True)
    c = v - mu
    var = jnp.mean(c * c, axis=-1, keepdims=True)
    return c * lax.rsqrt(var + LN_EPS) * g + b


def _ln1_body(alpha, y_ref, x_ref, m_ref, g_ref, b_ref, rw_ref, rb_ref, x1_ref, h2_ref, lg_ref):
    m = m_ref[0]
    x1 = _layernorm(alpha * x_ref[...] + m[2:3, :] * y_ref[...], g_ref[...], b_ref[...])
    x1_ref[...] = x1
    h2 = x1 * (1.0 + m[4:5, :]) + m[3:4, :]
    h2_ref[...] = h2
    lg_ref[...] = _dot3(h2, rw_ref[...]) + rb_ref[...]


def _ln1_router(y, xa, mod3, ln_g, ln_b, rw, rb, alpha, T, B):
    NA, D = xa.shape
    tr = _pick(NA, (256, 128))
    row = lambda i: (i, 0)
    fix = lambda i: (0, 0)
    return pl.pallas_call(
        functools.partial(_ln1_body, alpha),
        grid=(NA // tr,),
        in_specs=[pl.BlockSpec((tr, D), row), pl.BlockSpec((tr, D), row),
                  pl.BlockSpec((1, 6, D), _seg_map(tr, T, B)),
                  pl.BlockSpec((1, D), fix), pl.BlockSpec((1, D), fix),
                  pl.BlockSpec((D, LANES), fix), pl.BlockSpec((1, LANES), fix)],
        out_specs=[pl.BlockSpec((tr, D), row), pl.BlockSpec((tr, D), row), pl.BlockSpec((tr, LANES), row)],
        out_shape=[jax.ShapeDtypeStruct((NA, D), F32), jax.ShapeDtypeStruct((NA, D), F32),
                   jax.ShapeDtypeStruct((NA, LANES), F32)],
        compiler_params=_cparams(("arbitrary",)),
        name="ln1_router",
    )(y, xa, mod3, ln_g.reshape(1, D), ln_b.reshape(1, D), rw, rb)


def _row_copy(src_hbm, row, dst, j, sem):
    return pltpu.make_async_copy(src_hbm.at[pl.ds(row, 1)], dst.at[pl.ds(j, 1)], sem)


def _moe_gather_body(tm, src_ref, nv_ref, h_hbm, o_ref, buf, sem):
    i = pl.program_id(0)
    nv = nv_ref[0]
    slot = lax.rem(i, 2)

    def issue(tile, sl):
        def f(j, c):
            _row_copy(h_hbm, src_ref[tile * tm + j], buf.at[sl], j, sem.at[sl]).start()
            return c
        lax.fori_loop(0, tm, f, 0, unroll=DMA_UNROLL)

    def drain(tile, sl):
        pltpu.make_async_copy(h_hbm.at[pl.ds(0, tm)], buf.at[sl], sem.at[sl]).wait()

    @pl.when((i == 0) & (nv > 0))
    def _():
        issue(0, 0)

    @pl.when(i + 1 < nv)
    def _():
        issue(i + 1, 1 - slot)

    @pl.when(i < nv)
    def _():
        drain(i, slot)
        o_ref[...] = buf[slot].astype(o_ref.dtype)

    @pl.when(i >= nv)
    def _():
        o_ref[...] = jnp.zeros_like(o_ref)


def _moe_gather(h2, row_src, nvalid, tm):
    D = h2.shape[1]
    P = row_src.shape[0]
    return pl.pallas_call(
        functools.partial(_moe_gather_body, tm),
        grid_spec=pltpu.PrefetchScalarGridSpec(
            num_scalar_prefetch=2,
            grid=(P // tm,),
            in_specs=[pl.BlockSpec(memory_space=pl.ANY)],
            out_specs=pl.BlockSpec((tm, D), lambda i, src, nv: (i, 0)),
            scratch_shapes=[pltpu.VMEM((2, tm, D), F32), pltpu.SemaphoreType.DMA((2,))]),
        out_shape=jax.ShapeDtypeStruct((P, D), BF16),
        compiler_params=_cparams(("arbitrary",)),
        name="moe_gather",
    )(row_src, nvalid, h2)


def _moe_ffn_body(te_ref, nv_ref, x_ref, w1_ref, w3_ref, w2_ref, o_ref, w1b, w3b, w2b):
    i = pl.program_id(0)
    used = i < nv_ref[0]
    new_expert = (i == 0) | (te_ref[i] != te_ref[jnp.maximum(i - 1, 0)])

    @pl.when(used & new_expert)
    def _():
        w1b[...] = w1_ref[0, 0].astype(BF16)
        w3b[...] = w3_ref[0, 0].astype(BF16)
        w2b[...] = w2_ref[0, 0].astype(BF16)

    @pl.when(used)
    def _():
        x = x_ref[...]
        a = _silu(_dot(x, w1b[...])) * _dot(x, w3b[...])
        o_ref[...] = _dot(a.astype(BF16), w2b[...])

    @pl.when(jnp.logical_not(used))
    def _():
        o_ref[...] = jnp.zeros_like(o_ref)


def _moe_ffn(xs, tile_expert, nvalid, w1, w3, w2, layer, tm):
    P, D = xs.shape
    F = w1.shape[3]
    wmap = lambda i, te, nv: (layer, te[i], 0, 0)
    once = pl.Buffered(1)
    return pl.pallas_call(
        _moe_ffn_body,
        grid_spec=pltpu.PrefetchScalarGridSpec(
            num_scalar_prefetch=2,
            grid=(P // tm,),
            in_specs=[pl.BlockSpec((tm, D), lambda i, te, nv: (i, 0)),
                      pl.BlockSpec((1, 1, D, F), wmap, pipeline_mode=once),
                      pl.BlockSpec((1, 1, D, F), wmap, pipeline_mode=once),
                      pl.BlockSpec((1, 1, F, D), wmap, pipeline_mode=once)],
            out_specs=pl.BlockSpec((tm, D), lambda i, te, nv: (i, 0)),
            scratch_shapes=[pltpu.VMEM((D, F), BF16), pltpu.VMEM((D, F), BF16), pltpu.VMEM((F, D), BF16)]),
        out_shape=jax.ShapeDtypeStruct((P, D), F32),
        compiler_params=_cparams(("arbitrary",)),
        name="moe_ffn",
    )(tile_expert, nvalid, xs, w1, w3, w2)


def _combine_body(alpha, tr, has_next, pos_ref, ys_hbm, x_ref, gt_ref, m_ref, g_ref, b_ref, *rest):
    if has_next:
        mn_ref, x2_ref, hn_ref, ya, yb, sem = rest
    else:
        x2_ref, ya, yb, sem = rest
    i = pl.program_id(0)
    slot = lax.rem(i, 2)

    def copies(tile, sl, j):
        r = 2 * (tile * tr + j)
        return (_row_copy(ys_hbm, pos_ref[r], ya.at[sl], j, sem.at[sl]),
                _row_copy(ys_hbm, pos_ref[r + 1], yb.at[sl], j, sem.at[sl]))

    def issue(tile, sl):
        def f(j, c):
            for cp in copies(tile, sl, j):
                cp.start()
            return c
        lax.fori_loop(0, tr, f, 0, unroll=DMA_UNROLL)

    @pl.when(i == 0)
    def _():
        issue(0, 0)

    @pl.when(i + 1 < pl.num_programs(0))
    def _():
        issue(i + 1, 1 - slot)

    for dst in (ya, yb):
        pltpu.make_async_copy(ys_hbm.at[pl.ds(0, tr)], dst.at[slot], sem.at[slot]).wait()
    gt = gt_ref[...]
    y = gt[:, 0:1] * ya[slot] + gt[:, 1:2] * yb[slot]
    m = m_ref[0]
    x2 = _layernorm(alpha * x_ref[...] + m[5:6, :] * y, g_ref[...], b_ref[...])
    x2_ref[...] = x2
    if has_next:
        mn = mn_ref[0]
        hn_ref[...] = (x2 * (1.0 + mn[1:2, :]) + mn[0:1, :]).astype(hn_ref.dtype)


def _moe_combine(ys, pos, gates, x1, mod3, ln_g, ln_b, mod3_next, alpha, T, B):
    has_next = mod3_next is not None
    D = x1.shape[1]
    NA = x1.shape[0] if has_next else B * T
    tr = _pick(NA, (256, 128))
    row = lambda i, p: (i, 0)
    fix = lambda i, p: (0, 0)
    seg = _seg_map(tr, T, B)
    in_specs = [pl.BlockSpec(memory_space=pl.ANY),
                pl.BlockSpec((tr, D), row), pl.BlockSpec((tr, LANES), row),
                pl.BlockSpec((1, 6, D), seg),
                pl.BlockSpec((1, D), fix), pl.BlockSpec((1, D), fix)]
    args = [pos, ys, x1, gates, mod3, ln_g.reshape(1, D), ln_b.reshape(1, D)]
    out_specs = [pl.BlockSpec((tr, D), row)]
    out_shape = [jax.ShapeDtypeStruct((NA, D), F32)]
    if has_next:
        in_specs.append(pl.BlockSpec((1, 6, D), seg))
        args.append(mod3_next)
        out_specs.append(pl.BlockSpec((tr, D), row))
        out_shape.append(jax.ShapeDtypeStruct((NA, D), BF16))
    return pl.pallas_call(
        functools.partial(_combine_body, alpha, tr, has_next),
        grid_spec=pltpu.PrefetchScalarGridSpec(
            num_scalar_prefetch=1,
            grid=(NA // tr,),
            in_specs=in_specs,
            out_specs=out_specs,
            scratch_shapes=[pltpu.VMEM((2, tr, D), F32), pltpu.VMEM((2, tr, D), F32),
                            pltpu.SemaphoreType.DMA((2,))]),
        out_shape=out_shape,
        compiler_params=_cparams(("arbitrary",)),
        name="moe_combine",
    )(*args)


def _route(logits):
    lg = logits[:, :MOE_GROUPS]
    le_all = logits[:, MOE_GROUPS:MOE_GROUPS + MOE_EXPERTS]
    n = lg.shape[0]
    grp = jnp.argmax(lg, -1)
    p_grp = jnp.take_along_axis(jax.nn.softmax(lg, -1), grp[:, None], axis=1)
    le = jnp.take_along_axis(le_all.reshape(n, MOE_GROUPS, MOE_EPG), grp[:, None, None], axis=1)[:, 0]
    top_v, top_i = lax.top_k(le, 2)
    w_sel = p_grp * jax.nn.softmax(top_v, -1)
    eid = grp[:, None] * MOE_EPG + top_i
    return eid.astype(jnp.int32), w_sel


def _dispatch(eid, tm):
    flat = eid.reshape(-1)
    na = flat.shape[0]
    experts = jnp.arange(MOE_EXPERTS, dtype=jnp.int32)
    order = jnp.argsort(flat, stable=True).astype(jnp.int32)
    inv = jnp.argsort(order).astype(jnp.int32)
    counts = jnp.sum((flat[:, None] == experts[None, :]).astype(jnp.int32), axis=0)
    starts = jnp.cumsum(counts) - counts
    ptiles = (counts + tm - 1) // tm
    pstart_t = jnp.cumsum(ptiles) - ptiles
    pos = (pstart_t[flat] * tm + (inv - starts[flat])).astype(jnp.int32)
    nt = na // tm + MOE_EXPERTS
    nvalid = jnp.sum(ptiles).astype(jnp.int32)
    tile_ids = jnp.arange(nt, dtype=jnp.int32)
    tile_expert = jnp.sum((tile_ids[:, None] >= (pstart_t + ptiles)[None, :]).astype(jnp.int32), axis=1)
    tile_expert = jnp.minimum(tile_expert, MOE_EXPERTS - 1)
    last_e = tile_expert[jnp.maximum(nvalid - 1, 0)]
    tile_expert = jnp.where(tile_ids < nvalid, tile_expert, last_e)
    prow = jnp.arange(nt * tm, dtype=jnp.int32)
    pe = jnp.repeat(tile_expert, tm)
    within = jnp.minimum(prow - pstart_t[pe] * tm, jnp.maximum(counts[pe] - 1, 0))
    row_src = order[jnp.clip(starts[pe] + within, 0, na - 1)] // 2
    return row_src.astype(jnp.int32), pos, tile_expert.astype(jnp.int32), nvalid.reshape(1)


def _rope_tables(T):
    rows = T // GRID_W
    row = jnp.repeat(jnp.arange(rows, dtype=F32), GRID_W)
    col = jnp.tile(jnp.arange(GRID_W, dtype=F32), rows)
    axis_dim = HEAD_DIM // 2
    inv = ROPE_THETA ** (-jnp.arange(0, axis_dim, 2, dtype=F32) / axis_dim)
    ang = jnp.concatenate([row[:, None] * inv, col[:, None] * inv], -1)
    c, s = jnp.cos(ang), jnp.sin(ang)
    return jnp.concatenate([c, c], -1), jnp.concatenate([-s, s], -1)


def kernel(x, c, ctx, c_ctx, w_mod, b_mod, w_in, gdn_conv, gdn_a_log, gdn_dt_bias, gdn_norm, fn_w,
           q_norm, k_norm, w_out, ln1_g, ln1_b, ln2_g, ln2_b, router_g, router_g_b, router_e,
           router_e_b, w1, w3, w2):
    B, T, D = x.shape
    Tc = ctx.shape[1]
    L = w_mod.shape[0]
    NL, NC = B * T, B * Tc
    alpha = (2 * L) ** 0.25
    H = GDN_HEADS

    xa = jnp.concatenate([x.reshape(NL, D), ctx.reshape(NC, D)], axis=0)
    sc = jax.nn.silu(jnp.concatenate([c, c_ctx[None, :]], axis=0))
    sc8 = jnp.pad(sc, ((0, SUBLANES - B - 1), (0, 0)))
    mod = _modulation(sc8, w_mod, b_mod)
    cs_tab, sn_tab = _rope_tables(T)
    w1st, m2, m2c, cs_dft = _fn_tables(T, Tc)
    T1, T1c = T // LANES, Tc // LANES
    tm_moe = 256

    h = None
    for l in range(L):
        last = l == L - 1
        mod3 = mod[l].reshape(SUBLANES, 6, D)
        if h is None:
            h = _modcast(xa, mod3, T, B)
        wl = w_in[l]
        o = np.cumsum([0, 3 * GDN_WIDTH, GDN_WIDTH, 2 * H, 2 * H, FN_WIDTH, GQA_WIDTH, GQA_KV_WIDTH,
                       GQA_KV_WIDTH])
        w_main = jnp.concatenate([wl[:, o[0]:o[2]], wl[:, o[4]:o[8]]], axis=1).astype(BF16)
        wa, wb = wl[:, o[2]:o[3]], wl[:, o[3]:o[4]]
        zpad = jnp.zeros((D, LANES - 2 * H), F32)
        w_ab = jnp.concatenate([wa[:, :H], wb[:, :H], zpad, wa[:, H:], wb[:, H:], zpad], axis=1).astype(BF16)
        p_main = _mm([h], [w_main], F32, "in_proj")
        p_ab = _mm([h], [w_ab], F32, "in_proj_gates")
        qkvn = _gdn_inputs(p_main, gdn_conv[l], B, T, Tc)
        prow = jnp.pad(jnp.stack([gdn_a_log[l], gdn_dt_bias[l]], axis=1),
                       ((0, 0), (0, SUBLANES - 2), (0, LANES - H)))
        pcol = jnp.swapaxes(prow, 1, 2)
        p_abt = jnp.swapaxes(p_ab.reshape(-1, GDN_CHUNK, 2 * LANES), 1, 2)
        o_fwd, o_bwd = _gdn_scan(*_gdn_chunks(qkvn, p_ab, p_abt, prow, pcol), B, T, Tc)
        gdn_y = _gdn_output(o_fwd, o_bwd, p_main, gdn_norm[l])
        f = p_main[:, OFF_F:OFF_F + FN_WIDTH]
        fnw = fn_w[l].astype(BF16)
        z = _fn_stage1(f[:NL].reshape(B, T1, LANES * FN_WIDTH), w1st)
        fn_l = _fn_stage2(z.reshape(B, 2, T1, LANES, FN_WIDTH), m2, cs_dft, fnw, same_z=False)
        fn_c = _fn_stage2(f[NL:].reshape(B, 2, 1, LANES, FN_WIDTH), m2c, cs_dft, fnw, same_z=True)
        fn_y = jnp.concatenate([fn_l.reshape(NL, FN_WIDTH), fn_c.reshape(NC, FN_WIDTH)], axis=0)
        qk_att, vt_att = _att_prep(p_main, cs_tab, sn_tab, q_norm[l], k_norm[l], B, T)
        at_y = jnp.concatenate([_flash(qk_att, vt_att, B, T, Tc, True),
                                _flash(qk_att, vt_att, B, T, Tc, False)], axis=0)
        wo = w_out[l]
        ws = [wo[:GDN_WIDTH].astype(BF16), wo[GDN_WIDTH:GDN_WIDTH + FN_WIDTH].astype(BF16),
              wo[GDN_WIDTH + FN_WIDTH:].astype(BF16)]
        y = _mm([gdn_y, fn_y, at_y], ws, F32, "out_proj")
        rw = jnp.concatenate([router_g[l], router_e[l],
                              jnp.zeros((D, LANES - MOE_GROUPS - MOE_EXPERTS), F32)], axis=1)
        rb = jnp.concatenate([router_g_b[l], router_e_b[l],
                              jnp.zeros((LANES - MOE_GROUPS - MOE_EXPERTS,), F32)])[None, :]
        x1, h2, logits = _ln1_router(y, xa, mod3, ln1_g[l], ln1_b[l], rw, rb, alpha, T, B)
        eid, w_sel = _route(logits)
        row_src, pos, tile_expert, nvalid = _dispatch(eid, tm_moe)
        xs = _moe_gather(h2, row_src, nvalid, tm_moe)
        ys = _moe_ffn(xs, tile_expert, nvalid, w1, w3, w2, l, tm_moe)
        gates = jnp.pad(w_sel, ((0, 0), (0, LANES - 2)))
        mod3_next = None if last else mod[l + 1].reshape(SUBLANES, 6, D)
        res = _moe_combine(ys, pos, gates, x1, mod3, ln2_g[l], ln2_b[l], mod3_next, alpha, T, B)
        xa = res[0]
        h = None if last else res[1]
    return xa.reshape(B, T, D)
```
